```python
import jax
import jax.numpy as jnp
from jax import lax
import numpy as np

D_MODEL = 4096
BATCH = 4
SEQ = 2048
DEPTH = 4
DEC_BATCH = 8
DEC_SEQ = 8
PAST_LEN = 8192
PAGE_SIZE = 128

HEAD_DIM = 128
H_MOBA = 12
H_SB = 10
H_FOX = 10
H_TOT = H_MOBA + H_SB + H_FOX
N_BRANCH = 3
W_MIX = H_TOT * HEAD_DIM
D_FF = 5504
MOBA_BLOCK = 256
MOBA_TOPK = 3
MOBA_Q_BLOCK = 32
Q_BLOCK = 128
ROPE_THETA = 10000.0
NORM_EPS = 1e-6
FORGET_BIAS_INIT = 2.0
PROJ_SPLITS = (W_MIX, 2 * W_MIX, 3 * W_MIX, 3 * W_MIX + N_BRANCH * D_MODEL)
PROJ_COLS = 3 * W_MIX + N_BRANCH * D_MODEL + H_FOX
ATTN_SCALE = HEAD_DIM ** -0.5

kernel_name = "hybrid_moba_stickbreak_fox_step"


def _rmsnorm(x, g):
    xf = x.astype(jnp.float32)
    y = xf * lax.rsqrt(jnp.mean(xf * xf, axis=-1, keepdims=True) + NORM_EPS)
    return (y * g.astype(jnp.float32)).astype(x.dtype)


def _rope(x, pos):
    half = HEAD_DIM // 2
    inv = ROPE_THETA ** (-jnp.arange(half, dtype=jnp.float32) / half)
    ang = pos.astype(jnp.float32)[:, None] * inv[None, :]
    cos = jnp.cos(ang)[None, :, None, :]
    sin = jnp.sin(ang)[None, :, None, :]
    xf = x.astype(jnp.float32)
    x1, x2 = xf[..., :half], xf[..., half:]
    return jnp.concatenate([x1 * cos - x2 * sin, x1 * sin + x2 * cos], axis=-1).astype(x.dtype)


def _swiglu(x, w_gate, w_up, w_down):
    return (jax.nn.silu(x @ w_gate) * (x @ w_up)) @ w_down


def _block_size(t, qb):
    return qb if t % qb == 0 else t


def _sweep(fn, q, pos, qb):
    b, t, h, d = q.shape
    nb = t // qb
    qs = q.reshape(b, nb, qb, h, d).transpose(1, 0, 2, 3, 4)
    ps = pos.reshape(nb, qb)
    out = lax.map(lambda a: fn(a[0], a[1]), (qs, ps))
    return out.transpose(1, 0, 2, 3, 4).reshape(b, t, h, d)


def _moba(q, k, v, pos):
    b, l, h, d = k.shape
    nblk = -(-l // MOBA_BLOCK)
    pad = nblk * MOBA_BLOCK - l
    kb = jnp.pad(k, ((0, 0), (0, pad), (0, 0), (0, 0))).reshape(b, nblk, MOBA_BLOCK, h, d).transpose(0, 3, 1, 2, 4)
    vb = jnp.pad(v, ((0, 0), (0, pad), (0, 0), (0, 0))).reshape(b, nblk, MOBA_BLOCK, h, d).transpose(0, 3, 1, 2, 4)
    kmean = jnp.mean(kb.astype(jnp.float32), axis=3)
    n_sel = min(MOBA_TOPK, nblk)
    bi = jnp.arange(b)[:, None, None]
    hi = jnp.arange(h)[None, :, None]
    blk_ids = jnp.arange(nblk)
    offs = jnp.arange(MOBA_BLOCK)

    def blk(qc, pc):
        nq = qc.shape[1]
        qf = qc.astype(jnp.float32)
        cur = pc // MOBA_BLOCK
        gs = jnp.einsum('bqhd,bhnd->bhqn', qf, kmean)
        past_ok = blk_ids[None, :] < cur[:, None]
        gs = jnp.where(past_ok[None, None], gs, -jnp.inf)
        _, top = lax.top_k(gs, n_sel)
        sel_ok = top < cur[None, None, :, None]
        own = jnp.broadcast_to(cur[None, None, :, None], (b, h, nq, 1)).astype(top.dtype)
        idx = jnp.concatenate([top, own], axis=-1)
        ok = jnp.concatenate([sel_ok, jnp.ones(own.shape, dtype=bool)], axis=-1)
        flat = idx.reshape(b, h, -1)
        gk = kb[bi, hi, flat].reshape(b, h, nq, n_sel + 1, MOBA_BLOCK, d)
        gv = vb[bi, hi, flat].reshape(b, h, nq, n_sel + 1, MOBA_BLOCK, d)
        kp = idx[..., None] * MOBA_BLOCK + offs
        mask = ok[..., None] & (kp <= pc[None, None, :, None, None])
        s = jnp.einsum('bqhd,bhqjcd->bhqjc', qf, gk.astype(jnp.float32)) * ATTN_SCALE
        s = jnp.where(mask, s, -jnp.inf)
        p = jax.nn.softmax(s.reshape(b, h, nq, -1), axis=-1).reshape(s.shape)
        o = jnp.einsum('bhqjc,bhqjcd->bqhd', p, gv.astype(jnp.float32))
        return o.astype(qc.dtype)

    return _sweep(blk, q, pos, _block_size(q.shape[1], MOBA_Q_BLOCK))


def _stick_breaking(q, k, v, pos):
    kpos = jnp.arange(k.shape[1])
    kf = k.astype(jnp.float32)
    vf = v.astype(jnp.float32)

    def blk(qc, pc):
        z = jnp.einsum('bqhd,bkhd->bhqk', qc.astype(jnp.float32), kf) * ATTN_SCALE
        strict = (kpos[None, :] < pc[:, None])[None, None]
        lsn = jnp.where(strict, jax.nn.log_sigmoid(-z), 0.0)
        later = lax.cumsum(lsn, axis=3, reverse=True) - lsn
        w = jnp.where(strict, jnp.exp(jax.nn.log_sigmoid(z) + later), 0.0)
        return jnp.einsum('bhqk,bkhd->bqhd', w, vf).astype(qc.dtype)

    return _sweep(blk, q, pos, _block_size(q.shape[1], Q_BLOCK))


def _forgetting(q, k, v, logf, pos):
    kpos = jnp.arange(k.shape[1])
    kf = k.astype(jnp.float32)
    vf = v.astype(jnp.float32)
    fk = jnp.cumsum(logf.astype(jnp.float32), axis=1).transpose(0, 2, 1)

    def blk(qc, pc):
        s = jnp.einsum('bqhd,bkhd->bhqk', qc.astype(jnp.float32), kf) * ATTN_SCALE
        fq = jnp.take(fk, pc, axis=2)
        s = s + fq[..., None] - fk[:, :, None, :]
        causal = (kpos[None, :] <= pc[:, None])[None, None]
        p = jax.nn.softmax(jnp.where(causal, s, -jnp.inf), axis=-1)
        return jnp.einsum('bhqk,bkhd->bqhd', p, vf).astype(qc.dtype)

    return _sweep(blk, q, pos, _block_size(q.shape[1], Q_BLOCK))


def _layer(x, pos, past, w):
    (g_f1, f1_gate, f1_up, f1_down, g_mix, w_in, b_f, qg_a, kg_a, qg_c, kg_c,
     wb_a, wb_b, wb_c, w_out, g_f2, f2_gate, f2_up, f2_down) = w
    bsz, t, _ = x.shape
    h = x + 0.5 * _swiglu(_rmsnorm(x, g_f1), f1_gate, f1_up, f1_down)
    proj = _rmsnorm(h, g_mix) @ w_in
    q, k, v, gates, f_logit = jnp.split(proj, PROJ_SPLITS, axis=-1)
    q = q.reshape(bsz, t, H_TOT, HEAD_DIM)
    k = k.reshape(bsz, t, H_TOT, HEAD_DIM)
    v = v.reshape(bsz, t, H_TOT, HEAD_DIM)
    hs = (H_MOBA, H_MOBA + H_SB)
    q_a, q_b, q_c = jnp.split(q, hs, axis=2)
    k_a, k_b, k_c = jnp.split(k, hs, axis=2)
    q_a = _rope(_rmsnorm(q_a, qg_a), pos)
    k_a = _rope(_rmsnorm(k_a, kg_a), pos)
    q_c = _rmsnorm(q_c, qg_c)
    k_c = _rmsnorm(k_c, kg_c)
    logf = jax.nn.log_sigmoid((f_logit + b_f).astype(jnp.float32))
    k_rows = jnp.concatenate([k_a, k_b, k_c], axis=2)
    if past is None:
        k_all, v_all, logf_all = k_rows, v, logf
    else:
        k_all = jnp.concatenate([past[0], k_rows.astype(past[0].dtype)], axis=1)
        v_all = jnp.concatenate([past[1], v.astype(past[1].dtype)], axis=1)
        logf_all = jnp.concatenate([past[2].astype(jnp.float32), logf], axis=1)
    ka, kb_, kc = jnp.split(k_all, hs, axis=2)
    va, vb_, vc = jnp.split(v_all, hs, axis=2)
    o_a = _moba(q_a, ka, va, pos)
    o_b = _stick_breaking(q_b, kb_, vb_, pos)
    o_c = _forgetting(q_c, kc, vc, logf_all, pos)
    gate = jax.nn.sigmoid(gates.reshape(bsz, t, N_BRANCH, D_MODEL))
    merged = (gate[:, :, 0] * (o_a.reshape(bsz, t, -1) @ wb_a)
              + gate[:, :, 1] * (o_b.reshape(bsz, t, -1) @ wb_b)
              + gate[:, :, 2] * (o_c.reshape(bsz, t, -1) @ wb_c))
    h = h + merged @ w_out
    y = h + 0.5 * _swiglu(_rmsnorm(h, g_f2), f2_gate, f2_up, f2_down)
    return y, k_rows, v, logf


def _gather_past(pool, layer, page_table):
    rows = pool[layer, page_table]
    return rows.reshape((rows.shape[0], rows.shape[1] * rows.shape[2]) + rows.shape[3:])


def setup_inputs(seed: int = 0) -> dict:
    key = jax.random.key(seed)
    ks = jax.random.split(key, 32)
    f32 = jnp.float32
    n_pages = PAST_LEN // PAGE_SIZE
    n_used = DEC_BATCH * n_pages
    n_pool = n_used + (n_used + 3) // 4

    def nrm(k, shape, scale):
        return jax.random.normal(k, shape, f32) * scale

    def gain(k, shape):
        return 1.0 + 0.05 * jax.random.normal(k, shape, f32)

    x_prompt = jax.random.normal(ks[0], (BATCH, SEQ, D_MODEL), f32)
    x_sample = jax.random.normal(ks[1], (DEC_BATCH, DEC_SEQ, D_MODEL), f32)
    cache_k = jax.random.normal(ks[2], (DEPTH, n_pool, PAGE_SIZE, H_TOT, HEAD_DIM), f32)
    cache_v = jax.random.normal(ks[3], (DEPTH, n_pool, PAGE_SIZE, H_TOT, HEAD_DIM), f32)
    cache_logf = jax.nn.log_sigmoid(FORGET_BIAS_INIT + jax.random.normal(ks[4], (DEPTH, n_pool, PAGE_SIZE, H_FOX), f32))
    page_table = jax.random.permutation(ks[5], n_pool)[:n_used].reshape(DEC_BATCH, n_pages).astype(jnp.int32)
    d_in = D_MODEL ** -0.5
    return {
        "x_prompt": x_prompt,
        "x_sample": x_sample,
        "cache_k": cache_k,
        "cache_v": cache_v,
        "cache_logf": cache_logf,
        "page_table": page_table,
        "norm_ffn1": gain(ks[6], (DEPTH, D_MODEL)),
        "ffn1_w_gate": nrm(ks[7], (DEPTH, D_MODEL, D_FF), d_in),
        "ffn1_w_up": nrm(ks[8], (DEPTH, D_MODEL, D_FF), d_in),
        "ffn1_w_down": nrm(ks[9], (DEPTH, D_FF, D_MODEL), D_FF ** -0.5),
        "norm_mix": gain(ks[10], (DEPTH, D_MODEL)),
        "w_in": nrm(ks[11], (DEPTH, D_MODEL, PROJ_COLS), d_in),
        "b_forget": FORGET_BIAS_INIT + 0.1 * jax.random.normal(ks[12], (DEPTH, H_FOX), f32),
        "q_norm_moba": gain(ks[13], (DEPTH, HEAD_DIM)),
        "k_norm_moba": gain(ks[14], (DEPTH, HEAD_DIM)),
        "q_norm_fox": gain(ks[15], (DEPTH, HEAD_DIM)),
        "k_norm_fox": gain(ks[16], (DEPTH, HEAD_DIM)),
        "w_branch_moba": nrm(ks[17], (DEPTH, H_MOBA * HEAD_DIM, D_MODEL), (H_MOBA * HEAD_DIM) ** -0.5),
        "w_branch_sb": nrm(ks[18], (DEPTH, H_SB * HEAD_DIM, D_MODEL), (H_SB * HEAD_DIM) ** -0.5),
        "w_branch_fox": nrm(ks[19], (DEPTH, H_FOX * HEAD_DIM, D_MODEL), (H_FOX * HEAD_DIM) ** -0.5),
        "w_out": nrm(ks[20], (DEPTH, D_MODEL, D_MODEL), d_in),
        "norm_ffn2": gain(ks[21], (DEPTH, D_MODEL)),
        "ffn2_w_gate": nrm(ks[22], (DEPTH, D_MODEL, D_FF), d_in),
        "ffn2_w_up": nrm(ks[23], (DEPTH, D_MODEL, D_FF), d_in),
        "ffn2_w_down": nrm(ks[24], (DEPTH, D_FF, D_MODEL), D_FF ** -0.5),
    }


def reference(x_prompt, x_sample, cache_k, cache_v, cache_logf, page_table,
              norm_ffn1, ffn1_w_gate, ffn1_w_up, ffn1_w_down, norm_mix, w_in, b_forget,
              q_norm_moba, k_norm_moba, q_norm_fox, k_norm_fox,
              w_branch_moba, w_branch_sb, w_branch_fox, w_out,
              norm_ffn2, ffn2_w_gate, ffn2_w_up, ffn2_w_down):
    past_len = page_table.shape[1] * cache_k.shape[2]
    pos_p = jnp.arange(x_prompt.shape[1], dtype=jnp.int32)
    pos_s = past_len + jnp.arange(x_sample.shape[1], dtype=jnp.int32)
    yp, ys = x_prompt, x_sample
    kp_rows, vp_rows, lp_rows, ks_rows, vs_rows, ls_rows = [], [], [], [], [], []
    for l in range(DEPTH):
        w = (norm_ffn1[l], ffn1_w_gate[l], ffn1_w_up[l], ffn1_w_down[l], norm_mix[l], w_in[l], b_forget[l],
             q_norm_moba[l], k_norm_moba[l], q_norm_fox[l], k_norm_fox[l],
             w_branch_moba[l], w_branch_sb[l], w_branch_fox[l], w_out[l],
             norm_ffn2[l], ffn2_w_gate[l], ffn2_w_up[l], ffn2_w_down[l])
        yp, kp, vp, lp = _layer(yp, pos_p, None, w)
        past = (_gather_past(cache_k, l, page_table),
                _gather_past(cache_v, l, page_table),
                _gather_past(cache_logf, l, page_table))
        ys, ks_, vs_, ls_ = _layer(ys, pos_s, past, w)
        kp_rows.append(kp)
        vp_rows.append(vp)
        lp_rows.append(lp)
        ks_rows.append(ks_)
        vs_rows.append(vs_)
        ls_rows.append(ls_)
    return (yp, ys, jnp.stack(kp_rows), jnp.stack(vp_rows), jnp.stack(lp_rows),
            jnp.stack(ks_rows), jnp.stack(vs_rows), jnp.stack(ls_rows))
```

```python
import functools

import jax
import jax.numpy as jnp
from jax import lax
from jax.experimental import pallas as pl
from jax.experimental.pallas import tpu as pltpu

F32 = jnp.float32
BF16 = jnp.bfloat16

LANES = 128
SUBLANES = 8
VMEM_LIMIT_BYTES = 56 * 1024 * 1024

HEAD_DIM = 128
MOBA_BLOCK = 256
MOBA_TOPK = 3
ROPE_THETA = 10000.0
NORM_EPS = 1e-6
Q_TILE = 256
HEAD_GROUP = SUBLANES

_NT = (((1,), (1,)), ((), ()))
_HI = lax.Precision.HIGHEST


def _cparams(*sem):
    return pltpu.CompilerParams(dimension_semantics=sem, vmem_limit_bytes=VMEM_LIMIT_BYTES)


def _dot(a, b):
    return jnp.dot(a, b, preferred_element_type=F32)


def _dot_nt(a, b):
    return lax.dot_general(a, b, _NT, preferred_element_type=F32)


def _split3(x):
    hi = x.astype(BF16)
    r = x - hi.astype(F32)
    mid = r.astype(BF16)
    lo = (r - mid.astype(F32)).astype(BF16)
    return hi, mid, lo


def _tri_dot_left(tri, x):
    hi, mid, lo = _split3(x)
    return _dot(tri, hi) + _dot(tri, mid) + _dot(tri, lo)


def _tri_dot_right(x, tri):
    hi, mid, lo = _split3(x)
    return _dot(hi, tri) + _dot(mid, tri) + _dot(lo, tri)


def _log_sigmoid_pair(z):
    sp = jnp.log1p(jnp.exp(-jnp.abs(z)))
    return jnp.minimum(z, 0.0) - sp, jnp.minimum(-z, 0.0) - sp


def _tile(m, pref):
    t = pref
    while m % t:
        t //= 2
    assert t >= 1 and (t % SUBLANES == 0 or t == m), (m, pref)
    return min(t, m)


def _rmsnorm_body(x_ref, g_ref, o_ref):
    x = x_ref[...]
    ms = jnp.mean(x * x, axis=-1, keepdims=True)
    o_ref[...] = (x * lax.rsqrt(ms + NORM_EPS) * g_ref[...]).astype(o_ref.dtype)


def _rmsnorm(x, g):
    m, d = x.shape
    tm = _tile(m, 512)
    return pl.pallas_call(
        _rmsnorm_body,
        grid=(m // tm,),
        in_specs=[pl.BlockSpec((tm, d), lambda i: (i, 0)), pl.BlockSpec((1, d), lambda i: (0, 0))],
        out_specs=pl.BlockSpec((tm, d), lambda i: (i, 0)),
        out_shape=jax.ShapeDtypeStruct((m, d), BF16),
        compiler_params=_cparams("parallel"),
        name="rmsnorm",
    )(x, g.reshape(1, d))


def _ffn_up_body(x_ref, wg_ref, wu_ref, o_ref):
    x = x_ref[...]
    g = _dot(x, wg_ref[...])
    u = _dot(x, wu_ref[...])
    o_ref[...] = (g * jax.nn.sigmoid(g) * u).astype(o_ref.dtype)


def _ffn_up(xn, wg, wu):
    m, k = xn.shape
    n = wg.shape[1]
    tm, tn = _tile(m, 1024), _tile(n, 512)
    return pl.pallas_call(
        _ffn_up_body,
        grid=(m // tm, n // tn),
        in_specs=[pl.BlockSpec((tm, k), lambda i, j: (i, 0)),
                  pl.BlockSpec((k, tn), lambda i, j: (0, j)),
                  pl.BlockSpec((k, tn), lambda i, j: (0, j))],
        out_specs=pl.BlockSpec((tm, tn), lambda i, j: (i, j)),
        out_shape=jax.ShapeDtypeStruct((m, n), BF16),
        compiler_params=_cparams("parallel", "arbitrary"),
        name="ffn_up",
    )(xn, wg, wu)


def _mm_resid_body(x_ref, w_ref, r_ref, o_ref, *, scale):
    o_ref[...] = r_ref[...] + scale * _dot(x_ref[...], w_ref[...])


def _mm_resid(x, w, resid, scale):
    m, k = x.shape
    n = w.shape[1]
    tm, tn = _tile(m, 1024), _tile(n, 512)
    return pl.pallas_call(
        functools.partial(_mm_resid_body, scale=scale),
        grid=(m // tm, n // tn),
        in_specs=[pl.BlockSpec((tm, k), lambda i, j: (i, 0)),
                  pl.BlockSpec((k, tn), lambda i, j: (0, j)),
                  pl.BlockSpec((tm, tn), lambda i, j: (i, j))],
        out_specs=pl.BlockSpec((tm, tn), lambda i, j: (i, j)),
        out_shape=jax.ShapeDtypeStruct((m, n), F32),
        compiler_params=_cparams("parallel", "arbitrary"),
        name="mm_resid",
    )(x, w, resid)


PROJ_TN = 2 * HEAD_DIM


def _proj_qk_body(x_ref, w_ref, g_ref, cos_ref, sin_ref, *out_refs, n_rope, n_plain_end):
    j = pl.program_id(1)
    acc = _dot(x_ref[...], w_ref[...])

    def write(val):
        out_refs[0][...] = val.astype(BF16)
        if len(out_refs) > 1:
            out_refs[1][...] = val

    def normed(c):
        a = acc[:, c * HEAD_DIM:(c + 1) * HEAD_DIM]
        ms = jnp.mean(a * a, axis=-1, keepdims=True)
        return a * lax.rsqrt(ms + NORM_EPS) * g_ref[0]

    n_heads_tile = PROJ_TN // HEAD_DIM

    @pl.when(j < n_rope)
    def _():
        cos, sin = cos_ref[...], sin_ref[...]
        outs = []
        for c in range(n_heads_tile):
            a = normed(c)
            outs.append(a * cos + pltpu.roll(a, HEAD_DIM // 2, 1) * sin)
        write(jnp.concatenate(outs, axis=-1))

    @pl.when(jnp.logical_and(j >= n_rope, j < n_plain_end))
    def _():
        write(acc)

    @pl.when(j >= n_plain_end)
    def _():
        write(jnp.concatenate([normed(c) for c in range(n_heads_tile)], axis=-1))


def _proj_qk(xn, w_in, col_tile0, gains, cos_t, sin_t, heads, emit_f32):
    m, k = xn.shape
    h_moba, h_sb, h_fox = heads
    w_mix = (h_moba + h_sb + h_fox) * HEAD_DIM
    per = PROJ_TN // HEAD_DIM
    assert h_moba % per == 0 and h_sb % per == 0 and h_fox % per == 0
    n_rope, n_plain_end, n_tiles = h_moba // per, (h_moba + h_sb) // per, w_mix // PROJ_TN
    assert m % cos_t.shape[0] == 0
    tm = _tile(cos_t.shape[0], 1024)
    n_tbl = cos_t.shape[0] // tm
    out_shape = [jax.ShapeDtypeStruct((m, w_mix), BF16)]
    out_specs = [pl.BlockSpec((tm, PROJ_TN), lambda i, j: (i, j))]
    if emit_f32:
        out_shape.append(jax.ShapeDtypeStruct((m, w_mix), F32))
        out_specs.append(pl.BlockSpec((tm, PROJ_TN), lambda i, j: (i, j)))
    return pl.pallas_call(
        functools.partial(_proj_qk_body, n_rope=n_rope, n_plain_end=n_plain_end),
        grid=(m // tm, n_tiles),
        in_specs=[pl.BlockSpec((tm, k), lambda i, j: (i, 0)),
                  pl.BlockSpec((k, PROJ_TN), lambda i, j: (0, col_tile0 + j)),
                  pl.BlockSpec((None, 1, HEAD_DIM), lambda i, j: (jnp.where(j >= n_plain_end, 1, 0), 0, 0)),
                  pl.BlockSpec((tm, HEAD_DIM), lambda i, j: (i % n_tbl, 0)),
                  pl.BlockSpec((tm, HEAD_DIM), lambda i, j: (i % n_tbl, 0))],
        out_specs=out_specs,
        out_shape=out_shape,
        compiler_params=_cparams("parallel", "arbitrary"),
        name="proj_qk",
    )(xn, w_in, gains, cos_t, sin_t)


def _proj_v_body(x_ref, w_ref, o_bf_ref, o_f32_ref):
    acc = _dot(x_ref[...], w_ref[...])
    o_bf_ref[...] = acc.astype(BF16)
    o_f32_ref[...] = acc


def _proj_v(xn, w_in, col_tile0, n_cols):
    m, k = xn.shape
    tm = _tile(m, 1024)
    spec = pl.BlockSpec((tm, PROJ_TN), lambda i, j: (i, j))
    return pl.pallas_call(
        _proj_v_body,
        grid=(m // tm, n_cols // PROJ_TN),
        in_specs=[pl.BlockSpec((tm, k), lambda i, j: (i, 0)),
                  pl.BlockSpec((k, PROJ_TN), lambda i, j: (0, col_tile0 + j))],
        out_specs=[spec, spec],
        out_shape=[jax.ShapeDtypeStruct((m, n_cols), BF16), jax.ShapeDtypeStruct((m, n_cols), F32)],
        compiler_params=_cparams("parallel", "arbitrary"),
        name="proj_v",
    )(xn, w_in)


def _proj_gate_body(x_ref, w_ref, o_ref):
    o_ref[...] = jax.nn.sigmoid(_dot(x_ref[...], w_ref[...]))


def _proj_gate(xn, w_in, col_tile0, n_cols):
    m, k = xn.shape
    tm = _tile(m, 1024)
    return pl.pallas_call(
        _proj_gate_body,
        grid=(m // tm, n_cols // PROJ_TN),
        in_specs=[pl.BlockSpec((tm, k), lambda i, j: (i, 0)),
                  pl.BlockSpec((k, PROJ_TN), lambda i, j: (0, col_tile0 + j))],
        out_specs=pl.BlockSpec((tm, PROJ_TN), lambda i, j: (i, j)),
        out_shape=jax.ShapeDtypeStruct((m, n_cols), F32),
        compiler_params=_cparams("parallel", "arbitrary"),
        name="proj_gate",
    )(xn, w_in)


def _proj_logf_body(x_ref, w_ref, b_ref, o_ref, *, n_valid):
    acc = _dot(x_ref[...], w_ref[...])[:, :LANES]
    lane = lax.broadcasted_iota(jnp.int32, acc.shape, 1)
    ls, _ = _log_sigmoid_pair(acc + b_ref[...])
    o_ref[...] = jnp.where(lane < n_valid, ls, 0.0)


def _proj_logf(xn, w_in, col_tile0, b_pad, n_valid):
    m, k = xn.shape
    tm = _tile(m, 1024)
    return pl.pallas_call(
        functools.partial(_proj_logf_body, n_valid=n_valid),
        grid=(m // tm,),
        in_specs=[pl.BlockSpec((tm, k), lambda i: (i, 0)),
                  pl.BlockSpec((k, PROJ_TN), lambda i: (0, col_tile0)),
                  pl.BlockSpec((1, LANES), lambda i: (0, 0))],
        out_specs=pl.BlockSpec((tm, LANES), lambda i: (i, 0)),
        out_shape=jax.ShapeDtypeStruct((m, LANES), F32),
        compiler_params=_cparams("parallel"),
        name="proj_logf",
    )(xn, w_in, b_pad)


def _merge_body(oa_ref, ob_ref, oc_ref, wa_ref, wb_ref, wc_ref, ga_ref, gb_ref, gc_ref, o_ref):
    merged = (ga_ref[...] * _dot(oa_ref[...].astype(BF16), wa_ref[...])
              + gb_ref[...] * _dot(ob_ref[...].astype(BF16), wb_ref[...])
              + gc_ref[...] * _dot(oc_ref[...].astype(BF16), wc_ref[...]))
    o_ref[...] = merged.astype(o_ref.dtype)


def _merge(o_a, o_b, o_c, wb_a, wb_b, wb_c, gates):
    m = o_a.shape[0]
    d = wb_a.shape[1]
    tm, tn = _tile(m, 1024), _tile(d, 512)
    nd = d // tn

    def o_spec(o):
        return pl.BlockSpec((tm, o.shape[1]), lambda i, j: (i, 0))

    def w_spec(w):
        return pl.BlockSpec((w.shape[0], tn), lambda i, j: (0, j))

    def g_spec(b):
        return pl.BlockSpec((tm, tn), lambda i, j: (i, b * nd + j))

    return pl.pallas_call(
        _merge_body,
        grid=(m // tm, nd),
        in_specs=[o_spec(o_a), o_spec(o_b), o_spec(o_c), w_spec(wb_a), w_spec(wb_b), w_spec(wb_c),
                  g_spec(0), g_spec(1), g_spec(2)],
        out_specs=pl.BlockSpec((tm, tn), lambda i, j: (i, j)),
        out_shape=jax.ShapeDtypeStruct((m, d), BF16),
        compiler_params=_cparams("parallel", "arbitrary"),
        name="merge",
    )(o_a, o_b, o_c, wb_a, wb_b, wb_c, gates, gates, gates)


def _rope_table_body(pos_ref, inv_ref, cos_ref, sin_ref):
    ang = pos_ref[...] * inv_ref[...]
    lane = lax.broadcasted_iota(jnp.int32, ang.shape, 1)
    cos_ref[...] = jnp.cos(ang)
    sin_ref[...] = jnp.where(lane < HEAD_DIM // 2, -1.0, 1.0) * jnp.sin(ang)


def _rope_tables(pos):
    half = HEAD_DIM // 2
    inv = ROPE_THETA ** (-jnp.arange(half, dtype=F32) / half)
    inv_full = jnp.concatenate([inv, inv]).reshape(1, HEAD_DIM)
    r = pos.shape[0]
    spec = pl.BlockSpec((r, HEAD_DIM), lambda: (0, 0))
    return pl.pallas_call(
        _rope_table_body,
        in_specs=[pl.BlockSpec((r, 1), lambda: (0, 0)), pl.BlockSpec((1, HEAD_DIM), lambda: (0, 0))],
        out_specs=[spec, spec],
        out_shape=[jax.ShapeDtypeStruct((r, HEAD_DIM), F32)] * 2,
        name="rope_tables",
    )(pos.astype(F32).reshape(r, 1), inv_full)


def _tri(n, kind):
    r = lax.broadcasted_iota(jnp.int32, (n, n), 0)
    c = lax.broadcasted_iota(jnp.int32, (n, n), 1)
    return {"row_ge_col": r >= c, "row_gt_col": r > c}[kind].astype(BF16)


def _cumsum_prompt_body(x_ref, tri_ref, fcol_ref, frow_ref, *, n_chunks):
    tri = tri_ref[...]
    c_len = tri.shape[0]
    carry = jnp.zeros((1, LANES), F32)
    for c in range(n_chunks):
        f = _tri_dot_left(tri, x_ref[c * c_len:(c + 1) * c_len, :]) + carry
        fcol_ref[c * c_len:(c + 1) * c_len, :] = f
        frow_ref[:, c * c_len:(c + 1) * c_len] = f.T
        carry = f[c_len - 1:c_len, :]


def _cumsum_prompt(logf, batch, t):
    c_len = MOBA_BLOCK
    return pl.pallas_call(
        functools.partial(_cumsum_prompt_body, n_chunks=t // c_len),
        grid=(batch,),
        in_specs=[pl.BlockSpec((t, LANES), lambda b: (b, 0)),
                  pl.BlockSpec((c_len, c_len), lambda b: (0, 0))],
        out_specs=[pl.BlockSpec((t, LANES), lambda b: (b, 0)),
                   pl.BlockSpec((None, LANES, t), lambda b: (b, 0, 0))],
        out_shape=[jax.ShapeDtypeStruct((batch * t, LANES), F32),
                   jax.ShapeDtypeStruct((batch, LANES, t), F32)],
        compiler_params=_cparams("parallel"),
        name="cumsum_prompt",
    )(logf, _tri(c_len, "row_ge_col"))


def _cumsum_pages_body(pt_ref, x_ref, tri_ref, frow_ref, pad_ref, carry_ref, *, n_heads):
    p = pl.program_id(1)

    @pl.when(p == 0)
    def _():
        carry_ref[...] = jnp.zeros_like(carry_ref)

    pad_ref[...] = jnp.zeros_like(pad_ref)
    pad_ref[:, :n_heads] = x_ref[...]
    f = _tri_dot_left(tri_ref[...], pad_ref[...]) + carry_ref[...]
    frow_ref[...] = f.T
    carry_ref[...] = f[f.shape[0] - 1:, :]


def _cumsum_pages(cache_logf, layer, page_table):
    _, _, page, n_heads = cache_logf.shape
    db, n_pages = page_table.shape
    assert page == LANES
    grid_spec = pltpu.PrefetchScalarGridSpec(
        num_scalar_prefetch=1,
        grid=(db, n_pages),
        in_specs=[pl.BlockSpec((None, None, page, n_heads), lambda b, p, pt: (layer, pt[b, p], 0, 0)),
                  pl.BlockSpec((page, page), lambda b, p, pt: (0, 0))],
        out_specs=pl.BlockSpec((None, LANES, page), lambda b, p, pt: (b, 0, p)),
        scratch_shapes=[pltpu.VMEM((page, LANES), F32), pltpu.VMEM((1, LANES), F32)],
    )
    return pl.pallas_call(
        functools.partial(_cumsum_pages_body, n_heads=n_heads),
        grid_spec=grid_spec,
        out_shape=jax.ShapeDtypeStruct((db, LANES, n_pages * page), F32),
        compiler_params=_cparams("parallel", "arbitrary"),
        name="cumsum_pages",
    )(page_table, cache_logf, _tri(page, "row_ge_col"))


def _causal_iotas(n):
    row = lax.broadcasted_iota(jnp.int32, (n, n), 0)
    col = lax.broadcasted_iota(jnp.int32, (n, n), 1)
    return row, col


def _moba_select(q, kmean, i, nb):
    shift = max(nb - 1, 1).bit_length()
    width = 1 << shift
    assert width * width <= LANES

    def hi_part(x):
        return lax.shift_right_logical(x, shift)

    def lo_part(x):
        return jnp.bitwise_and(x, width - 1)

    gs = lax.dot_general(q.astype(F32), kmean, _NT, precision=_HI, preferred_element_type=F32)
    r = lax.broadcasted_iota(jnp.int32, (LANES, LANES), 0)
    c = lax.broadcasted_iota(jnp.int32, (LANES, LANES), 1)
    pair_ok = c < width * width
    diff_mat = jnp.where(pair_ok, jnp.where(r == hi_part(c), 1.0, 0.0) - jnp.where(r == lo_part(c), 1.0, 0.0), 0.0)
    diff = jnp.dot(gs, diff_mat, precision=_HI, preferred_element_type=F32)
    lane = lax.broadcasted_iota(jnp.int32, diff.shape, 1)
    n_l, m_l = hi_part(lane), lo_part(lane)
    beats = jnp.logical_or(diff < 0.0, jnp.logical_and(diff == 0.0, m_l < n_l))
    beats = jnp.logical_and(beats, jnp.logical_and(m_l < i, lane < width * width))
    sum_mat = jnp.where(jnp.logical_and(hi_part(r) == c, r < width * width), 1.0, 0.0).astype(BF16)
    rank = _dot(jnp.where(beats, 1.0, 0.0).astype(BF16), sum_mat)
    return jnp.where(jnp.logical_and(rank < MOBA_TOPK, lane < i), 1.0, 0.0)


def _moba_prompt_body(q_ref, k_ref, v_ref, o_ref, kmean_ref, *, nb, scale):
    i = pl.program_id(2)
    blk = MOBA_BLOCK

    @pl.when(i == 0)
    def _():
        kmean_ref[...] = jnp.zeros_like(kmean_ref)
        for n in range(nb):
            kmean_ref[n:n + 1, :] = jnp.mean(k_ref[n * blk:(n + 1) * blk, :].astype(F32), axis=0, keepdims=True)

    q = q_ref[...]
    sel = _moba_select(q, kmean_ref[...], i, nb)
    lane = lax.broadcasted_iota(jnp.int32, sel.shape, 1)

    start = pl.multiple_of(i * blk, blk)
    row, col = _causal_iotas(blk)
    s = jnp.where(col <= row, _dot_nt(q, k_ref[pl.ds(start, blk), :]) * scale, -jnp.inf)
    m0 = jnp.max(s, axis=-1, keepdims=True)
    p = jnp.exp(s - m0)
    l0 = jnp.sum(p, axis=-1, keepdims=True)
    acc0 = _dot(p.astype(BF16), v_ref[pl.ds(start, blk), :])

    def past(n, carry):
        m, l, acc = carry
        st = pl.multiple_of(n * blk, blk)
        chosen = jnp.sum(jnp.where(lane == n, sel, 0.0), axis=-1, keepdims=True) > 0.5
        s = jnp.where(chosen, _dot_nt(q, k_ref[pl.ds(st, blk), :]) * scale, -jnp.inf)
        m_new = jnp.maximum(m, jnp.max(s, axis=-1, keepdims=True))
        alpha = jnp.exp(m - m_new)
        p = jnp.exp(s - m_new)
        l = alpha * l + jnp.sum(p, axis=-1, keepdims=True)
        acc = alpha * acc + _dot(p.astype(BF16), v_ref[pl.ds(st, blk), :])
        return m_new, l, acc

    _, l, acc = lax.fori_loop(0, i, past, (m0, l0, acc0))
    o_ref[...] = (acc / l).astype(o_ref.dtype)


def _sb_prompt_body(q_ref, k_ref, v_ref, tri_ref, o_ref, *, scale):
    i = pl.program_id(2)
    blk = Q_TILE
    q = q_ref[...]
    tri = tri_ref[...]

    def block(st, carry, acc, strict):
        z = _dot_nt(q, k_ref[pl.ds(st, blk), :]) * scale
        ls, lsn = _log_sigmoid_pair(z)
        if strict is not None:
            lsn = jnp.where(strict, lsn, 0.0)
        later = _tri_dot_right(lsn, tri)
        w = jnp.exp(ls + later + carry)
        if strict is not None:
            w = jnp.where(strict, w, 0.0)
        acc = acc + _dot(w.astype(BF16), v_ref[pl.ds(st, blk), :])
        return carry + jnp.sum(lsn, axis=-1, keepdims=True), acc

    row, col = _causal_iotas(blk)
    carry0, acc0 = block(pl.multiple_of(i * blk, blk), jnp.zeros((blk, 1), F32),
                         jnp.zeros((blk, HEAD_DIM), F32), col < row)

    def past(t, c):
        return block(pl.multiple_of((i - 1 - t) * blk, blk), c[0], c[1], None)

    _, acc = lax.fori_loop(0, i, past, (carry0, acc0))
    o_ref[...] = acc.astype(o_ref.dtype)


def _fox_prompt_body(q_ref, k_ref, v_ref, fcol_ref, frow_ref, o_ref, *, scale):
    h = pl.program_id(1)
    i = pl.program_id(2)
    blk = Q_TILE
    q = q_ref[...]
    fcol = fcol_ref[...]
    lane = lax.broadcasted_iota(jnp.int32, fcol.shape, 1)
    fq = jnp.sum(jnp.where(lane == h, fcol, 0.0), axis=-1, keepdims=True)

    def scores(st):
        fk = frow_ref[pl.ds(h, 1), pl.ds(st, blk)]
        return _dot_nt(q, k_ref[pl.ds(st, blk), :]) * scale + (fq - fk)

    start = pl.multiple_of(i * blk, blk)
    row, col = _causal_iotas(blk)
    s = jnp.where(col <= row, scores(start), -jnp.inf)
    m0 = jnp.max(s, axis=-1, keepdims=True)
    p = jnp.exp(s - m0)
    l0 = jnp.sum(p, axis=-1, keepdims=True)
    acc0 = _dot(p.astype(BF16), v_ref[pl.ds(start, blk), :])

    def past(n, carry):
        m, l, acc = carry
        st = pl.multiple_of(n * blk, blk)
        s = scores(st)
        m_new = jnp.maximum(m, jnp.max(s, axis=-1, keepdims=True))
        alpha = jnp.exp(m - m_new)
        p = jnp.exp(s - m_new)
        l = alpha * l + jnp.sum(p, axis=-1, keepdims=True)
        acc = alpha * acc + _dot(p.astype(BF16), v_ref[pl.ds(st, blk), :])
        return m_new, l, acc

    _, l, acc = lax.fori_loop(0, i, past, (m0, l0, acc0))
    o_ref[...] = (acc / l).astype(o_ref.dtype)


def _prompt_attention(q_bf, k_bf, v_bf, fcol, frow, batch, t, heads):
    h_moba, h_sb, h_fox = heads
    assert t % Q_TILE == 0 and Q_TILE == MOBA_BLOCK
    nq = t // Q_TILE
    m = batch * t
    scale = HEAD_DIM ** -0.5
    assert nq * nq <= LANES

    def specs(h0):
        qs = pl.BlockSpec((Q_TILE, HEAD_DIM), lambda b, h, i: (b * nq + i, h0 + h))
        ks = pl.BlockSpec((t, HEAD_DIM), lambda b, h, i: (b, h0 + h))
        return qs, ks, ks

    def out(n_heads):
        return (pl.BlockSpec((Q_TILE, HEAD_DIM), lambda b, h, i: (b * nq + i, h)),
                jax.ShapeDtypeStruct((m, n_heads * HEAD_DIM), BF16))

    sem = _cparams("parallel", "parallel", "arbitrary")

    o_spec, o_shape = out(h_moba)
    o_a = pl.pallas_call(
        functools.partial(_moba_prompt_body, nb=nq, scale=scale),
        grid=(batch, h_moba, nq),
        in_specs=list(specs(0)),
        out_specs=o_spec, out_shape=o_shape,
        scratch_shapes=[pltpu.VMEM((LANES, HEAD_DIM), F32)],
        compiler_params=sem, name="moba_prompt",
    )(q_bf, k_bf, v_bf)

    o_spec, o_shape = out(h_sb)
    o_b = pl.pallas_call(
        functools.partial(_sb_prompt_body, scale=scale),
        grid=(batch, h_sb, nq),
        in_specs=list(specs(h_moba)) + [pl.BlockSpec((Q_TILE, Q_TILE), lambda b, h, i: (0, 0))],
        out_specs=o_spec, out_shape=o_shape,
        compiler_params=sem, name="sb_prompt",
    )(q_bf, k_bf, v_bf, _tri(Q_TILE, "row_gt_col"))

    o_spec, o_shape = out(h_fox)
    o_c = pl.pallas_call(
        functools.partial(_fox_prompt_body, scale=scale),
        grid=(batch, h_fox, nq),
        in_specs=list(specs(h_moba + h_sb)) + [
            pl.BlockSpec((Q_TILE, LANES), lambda b, h, i: (b * nq + i, 0)),
            pl.BlockSpec((None, LANES, t), lambda b, h, i: (b, 0, 0))],
        out_specs=o_spec, out_shape=o_shape,
        compiler_params=sem, name="fox_prompt",
    )(q_bf, k_bf, v_bf, fcol, frow)
    return o_a, o_b, o_c


def _sample_attn_body(pt_ref, q_ref, kn_ref, vn_ref, lfn_ref, frow_ref, tri_ref, *refs,
                      heads, n_groups, n_pages, scale, n_new):
    h_moba, h_sb, h_fox = heads
    h_tot = h_moba + h_sb + h_fox
    k_refs = refs[:n_groups]
    v_refs = refs[n_groups:2 * n_groups]
    o_ref = refs[2 * n_groups]
    (kpad_ref, vpad_ref, acc_ref, m_ref, l_ref, carry_ref, ftot_ref, cumn_ref, cumn_t_ref,
     pm_ref, pl_ref, po_ref, pg_ref) = refs[2 * n_groups + 1:]

    r = pl.program_id(1)
    page = LANES
    rows = q_ref.shape[0]
    pages_per_block = MOBA_BLOCK // page

    def q_head(h):
        return q_ref[:, h * HEAD_DIM:(h + 1) * HEAD_DIM].astype(BF16)

    def bcast(x):
        return jnp.broadcast_to(x, (rows, LANES))

    @pl.when(r == 0)
    def _():
        kpad_ref[...] = jnp.zeros_like(kpad_ref)
        vpad_ref[...] = jnp.zeros_like(vpad_ref)
        kpad_ref[0:n_new, :] = kn_ref[...]
        vpad_ref[0:n_new, :] = vn_ref[...]
        row = lax.broadcasted_iota(jnp.int32, (rows, page), 0)
        col = lax.broadcasted_iota(jnp.int32, (rows, page), 1)
        weak = col <= row
        strict = col < row

        cumn_t_ref[...] = jnp.zeros_like(cumn_t_ref)
        cumn_t_ref[0:rows, :] = lfn_ref[...]
        lfn_pad = cumn_t_ref[...]
        cum_pad = _tri_dot_left(tri_ref[...], lfn_pad) + lfn_pad
        cumn = cum_pad[0:rows, :]
        cumn_ref[...] = cumn
        cumn_t_ref[...] = cum_pad.T
        last = frow_ref[:, page - 1:page]
        ftot_ref[...] = jnp.broadcast_to(last, ftot_ref.shape)

        for h in range(h_tot):
            kh = kpad_ref[:, h * HEAD_DIM:(h + 1) * HEAD_DIM].astype(BF16)
            vh = vpad_ref[:, h * HEAD_DIM:(h + 1) * HEAD_DIM].astype(BF16)
            s_raw = _dot_nt(q_head(h), kh)
            if h < h_moba:
                s = jnp.where(weak, s_raw * scale, -jnp.inf)
                m = jnp.max(s, axis=-1, keepdims=True)
                p = jnp.exp(s - m)
                pm_ref[n_pages, h] = bcast(m)
                pl_ref[n_pages, h] = bcast(jnp.sum(p, axis=-1, keepdims=True))
                po_ref[n_pages, h] = _dot(p.astype(BF16), vh)
            elif h < h_moba + h_sb:
                ls, lsn = _log_sigmoid_pair(s_raw * scale)
                lsn = jnp.where(strict, lsn, 0.0)
                later = _tri_dot_right(lsn, tri_ref[...])
                w = jnp.where(strict, jnp.exp(ls + later), 0.0)
                acc_ref[h] = _dot(w.astype(BF16), vh)
                carry_ref[h - h_moba] = bcast(jnp.sum(lsn, axis=-1, keepdims=True))
            else:
                c = h - h_moba - h_sb
                bias = cumn[:, c:c + 1] - cumn_t_ref[c:c + 1, :]
                s = jnp.where(weak, s_raw * scale + bias, -jnp.inf)
                m = jnp.max(s, axis=-1, keepdims=True)
                p = jnp.exp(s - m)
                m_ref[c] = bcast(m)
                l_ref[c] = bcast(jnp.sum(p, axis=-1, keepdims=True))
                acc_ref[h] = _dot(p.astype(BF16), vh)

    for h in range(h_tot):
        g, hg = divmod(h, HEAD_GROUP)
        kh = k_refs[g][:, hg, :].astype(BF16)
        vh = v_refs[g][:, hg, :].astype(BF16)
        s_raw = _dot_nt(q_head(h), kh)
        if h < h_moba:
            s = s_raw * scale
            m = jnp.max(s, axis=-1, keepdims=True)
            p = jnp.exp(s - m)
            pm_ref[r, h] = bcast(m)
            pl_ref[r, h] = bcast(jnp.sum(p, axis=-1, keepdims=True))
            po_ref[r, h] = _dot(p.astype(BF16), vh)
            pg_ref[r, h] = bcast(jnp.sum(s_raw, axis=-1, keepdims=True))
        elif h < h_moba + h_sb:
            c = h - h_moba
            ls, lsn = _log_sigmoid_pair(s_raw * scale)
            later = _tri_dot_right(lsn, tri_ref[...])
            w = jnp.exp(ls + later + carry_ref[c])
            acc_ref[h] = acc_ref[h] + _dot(w.astype(BF16), vh)
            carry_ref[c] = carry_ref[c] + bcast(jnp.sum(lsn, axis=-1, keepdims=True))
        else:
            c = h - h_moba - h_sb
            fq = cumn_ref[:, c:c + 1] + ftot_ref[c:c + 1, :]
            s = s_raw * scale + (fq - frow_ref[c:c + 1, :])
            m_old = m_ref[c]
            m_new = jnp.maximum(m_old, bcast(jnp.max(s, axis=-1, keepdims=True)))
            alpha = jnp.exp(m_old - m_new)
            p = jnp.exp(s - m_new)
            l_ref[c] = alpha * l_ref[c] + bcast(jnp.sum(p, axis=-1, keepdims=True))
            acc_ref[h] = alpha * acc_ref[h] + _dot(p.astype(BF16), vh)
            m_ref[c] = m_new

    @pl.when(r == n_pages - 1)
    def _():
        n_blocks = n_pages // pages_per_block

        def block_score(h, n):
            tot = pg_ref[n_pages - 1 - n * pages_per_block, h]
            for e in range(1, pages_per_block):
                tot = tot + pg_ref[n_pages - 1 - (n * pages_per_block + e), h]
            return tot * (1.0 / MOBA_BLOCK)

        for h in range(h_tot):
            if h < h_moba:
                m_run = pm_ref[n_pages, h]
                l_run = pl_ref[n_pages, h]
                o_run = po_ref[n_pages, h]

                def add_block(n, carry, h=h):
                    m_run, l_run, o_run = carry
                    gs_n = block_score(h, n)

                    def count(mb, rank):
                        gs_m = block_score(h, mb)
                        beats = jnp.logical_or(gs_m > gs_n, jnp.logical_and(gs_m == gs_n, mb < n))
                        return rank + jnp.where(beats, 1.0, 0.0)

                    rank = lax.fori_loop(0, n_blocks, count, jnp.zeros((rows, LANES), F32))
                    chosen = rank < MOBA_TOPK
                    for e in range(pages_per_block):
                        slot = n_pages - 1 - (n * pages_per_block + e)
                        m_p = jnp.where(chosen, pm_ref[slot, h], -jnp.inf)
                        m_new = jnp.maximum(m_run, m_p)
                        a_run = jnp.exp(m_run - m_new)
                        a_p = jnp.exp(m_p - m_new)
                        l_run = a_run * l_run + a_p * pl_ref[slot, h]
                        o_run = a_run * o_run + a_p * po_ref[slot, h]
                        m_run = m_new
                    return m_run, l_run, o_run

                _, l_run, o_run = lax.fori_loop(0, n_blocks, add_block, (m_run, l_run, o_run))
                res = o_run / l_run
            elif h < h_moba + h_sb:
                res = acc_ref[h]
            else:
                res = acc_ref[h] / l_ref[h - h_moba - h_sb]
            o_ref[:, h * HEAD_DIM:(h + 1) * HEAD_DIM] = res


def _sample_attention(q_s, k_new, v_new, logf_new, frow_past, cache_k, cache_v, layer, page_table, heads):
    h_moba, h_sb, h_fox = heads
    h_tot = h_moba + h_sb + h_fox
    db, n_pages = page_table.shape
    n_new = q_s.shape[0] // db
    _, n_pool, page, hh, dd = cache_k.shape
    assert (page, hh, dd) == (LANES, h_tot, HEAD_DIM) and h_tot % HEAD_GROUP == 0
    assert n_new == SUBLANES and (n_pages * page) % MOBA_BLOCK == 0 and MOBA_BLOCK % page == 0
    n_groups = h_tot // HEAD_GROUP
    w = h_tot * HEAD_DIM

    def row_spec(width):
        return pl.BlockSpec((n_new, width), lambda b, r, pt: (b, 0))

    def cache_spec(g):
        return pl.BlockSpec((None, None, page, HEAD_GROUP, HEAD_DIM),
                            lambda b, r, pt: (layer, pt[b, n_pages - 1 - r], 0, g, 0))

    grid_spec = pltpu.PrefetchScalarGridSpec(
        num_scalar_prefetch=1,
        grid=(db, n_pages),
        in_specs=[row_spec(w), row_spec(w), row_spec(w), row_spec(LANES),
                  pl.BlockSpec((None, LANES, page), lambda b, r, pt: (b, 0, n_pages - 1 - r)),
                  pl.BlockSpec((page, page), lambda b, r, pt: (0, 0))]
                 + [cache_spec(g) for g in range(n_groups)] * 2,
        out_specs=row_spec(w),
        scratch_shapes=[
            pltpu.VMEM((page, w), F32), pltpu.VMEM((page, w), F32),
            pltpu.VMEM((h_tot, n_new, HEAD_DIM), F32),
            pltpu.VMEM((h_fox, n_new, LANES), F32),
            pltpu.VMEM((h_fox, n_new, LANES), F32),
            pltpu.VMEM((h_sb, n_new, LANES), F32),
            pltpu.VMEM((LANES, LANES), F32),
            pltpu.VMEM((n_new, LANES), F32),
            pltpu.VMEM((LANES, LANES), F32),
            pltpu.VMEM((n_pages + 1, h_moba, n_new, LANES), F32),
            pltpu.VMEM((n_pages + 1, h_moba, n_new, LANES), F32),
            pltpu.VMEM((n_pages + 1, h_moba, n_new, HEAD_DIM), F32),
            pltpu.VMEM((n_pages, h_moba, n_new, LANES), F32),
        ],
    )
    body = functools.partial(_sample_attn_body, heads=heads, n_groups=n_groups, n_pages=n_pages,
                             scale=HEAD_DIM ** -0.5, n_new=n_new)
    return pl.pallas_call(
        body,
        grid_spec=grid_spec,
        out_shape=jax.ShapeDtypeStruct((db * n_new, w), F32),
        compiler_params=_cparams("parallel", "arbitrary"),
        name="sample_attn",
    )(page_table, q_s, k_new, v_new, logf_new, frow_past, _tri(page, "row_gt_col"),
      *([cache_k] * n_groups), *([cache_v] * n_groups))


def _layer_common(x, w, heads, cos_t, sin_t):
    h = _mm_resid(_ffn_up(_rmsnorm(x, w["g_f1"]), w["f1_gate"], w["f1_up"]), w["f1_down"], x, 0.5)
    n = _rmsnorm(h, w["g_mix"])
    w_mix = sum(heads) * HEAD_DIM
    d = x.shape[1]
    t0 = w_mix // PROJ_TN
    q_bf, = _proj_qk(n, w["w_in"], 0, w["q_gains"], cos_t, sin_t, heads, False)
    k_bf, k_f32 = _proj_qk(n, w["w_in"], t0, w["k_gains"], cos_t, sin_t, heads, True)
    v_bf, v_f32 = _proj_v(n, w["w_in"], 2 * t0, w_mix)
    gates = _proj_gate(n, w["w_in"], 3 * t0, 3 * d)
    logf = _proj_logf(n, w["w_in"], 3 * t0 + 3 * d // PROJ_TN, w["b_f"], heads[2])
    return h, q_bf, k_bf, k_f32, v_bf, v_f32, gates, logf


def _layer_tail(h, o_a, o_b, o_c, gates, w):
    merged = _merge(o_a, o_b, o_c, w["wb_a"], w["wb_b"], w["wb_c"], gates)
    h = _mm_resid(merged, w["w_out"], h, 1.0)
    return _mm_resid(_ffn_up(_rmsnorm(h, w["g_f2"]), w["f2_gate"], w["f2_up"]), w["f2_down"], h, 0.5)


def kernel(x_prompt, x_sample, cache_k, cache_v, cache_logf, page_table, norm_ffn1, ffn1_w_gate, ffn1_w_up, ffn1_w_down, norm_mix, w_in, b_forget, q_norm_moba, k_norm_moba, q_norm_fox, k_norm_fox, w_branch_moba, w_branch_sb, w_branch_fox, w_out, norm_ffn2, ffn2_w_gate, ffn2_w_up, ffn2_w_down):
    batch, t, d = x_prompt.shape
    db, n_new, _ = x_sample.shape
    depth = norm_ffn1.shape[0]
    h_fox = b_forget.shape[1]
    h_moba = w_branch_moba.shape[1] // HEAD_DIM
    h_sb = w_branch_sb.shape[1] // HEAD_DIM
    heads = (h_moba, h_sb, h_fox)
    h_tot = sum(heads)
    w_mix = h_tot * HEAD_DIM
    past_len = page_table.shape[1] * cache_k.shape[2]
    assert past_len % MOBA_BLOCK == 0 and n_new <= MOBA_BLOCK
    assert (3 * w_mix + 3 * d) % PROJ_TN == 0 and h_fox <= LANES

    d_ff = ffn1_w_gate.shape[2]
    ff_pad = -d_ff % 512

    def up(wt):
        return jnp.pad(wt.astype(BF16), ((0, 0), (0, 0), (0, ff_pad)))

    def down(wt):
        return jnp.pad(wt.astype(BF16), ((0, 0), (0, ff_pad), (0, 0)))

    f1g, f1u, f1d = up(ffn1_w_gate), up(ffn1_w_up), down(ffn1_w_down)
    f2g, f2u, f2d = up(ffn2_w_gate), up(ffn2_w_up), down(ffn2_w_down)
    w_in_b = w_in.astype(BF16)
    wba, wbb, wbc, wo = (a.astype(BF16) for a in (w_branch_moba, w_branch_sb, w_branch_fox, w_out))
    b_pad = jnp.pad(b_forget, ((0, 0), (0, LANES - h_fox)))

    cos_p, sin_p = _rope_tables(jnp.arange(t, dtype=jnp.int32))
    cos_s, sin_s = _rope_tables(past_len + jnp.arange(n_new, dtype=jnp.int32))
    cos_s, sin_s = jnp.tile(cos_s, (db, 1)), jnp.tile(sin_s, (db, 1))

    yp = x_prompt.reshape(batch * t, d)
    ys = x_sample.reshape(db * n_new, d)
    outs = [[] for _ in range(6)]
    for l in range(depth):
        w = dict(g_f1=norm_ffn1[l], f1_gate=f1g[l], f1_up=f1u[l], f1_down=f1d[l], g_mix=norm_mix[l],
                 w_in=w_in_b[l], b_f=b_pad[l:l + 1],
                 q_gains=jnp.stack([q_norm_moba[l], q_norm_fox[l]]).reshape(2, 1, HEAD_DIM),
                 k_gains=jnp.stack([k_norm_moba[l], k_norm_fox[l]]).reshape(2, 1, HEAD_DIM),
                 wb_a=wba[l], wb_b=wbb[l], wb_c=wbc[l], w_out=wo[l],
                 g_f2=norm_ffn2[l], f2_gate=f2g[l], f2_up=f2u[l], f2_down=f2d[l])

        h, q_bf, k_bf, k_f32, v_bf, v_f32, gates, logf = _layer_common(yp, w, heads, cos_p, sin_p)
        fcol, frow = _cumsum_prompt(logf, batch, t)
        o_a, o_b, o_c = _prompt_attention(q_bf, k_bf, v_bf, fcol, frow, batch, t, heads)
        yp = _layer_tail(h, o_a, o_b, o_c, gates, w)
        outs[0].append(k_f32.reshape(batch, t, h_tot, HEAD_DIM))
        outs[1].append(v_f32.reshape(batch, t, h_tot, HEAD_DIM))
        outs[2].append(logf[:, :h_fox].reshape(batch, t, h_fox))

        h, q_bf, k_bf, k_f32, v_bf, v_f32, gates, logf = _layer_common(ys, w, heads, cos_s, sin_s)
        frow_past = _cumsum_pages(cache_logf, l, page_table)
        o = _sample_attention(q_bf.astype(F32), k_f32, v_f32, logf, frow_past, cache_k, cache_v, l,
                              page_table, heads)
        a_end, b_end = h_moba * HEAD_DIM, (h_moba + h_sb) * HEAD_DIM
        ys = _layer_tail(h, o[:, :a_end], o[:, a_end:b_end], o[:, b_end:], gates, w)
        outs[3].append(k_f32.reshape(db, n_new, h_tot, HEAD_DIM))
        outs[4].append(v_f32.reshape(db, n_new, h_tot, HEAD_DIM))
        outs[5].append(logf[:, :h_fox].reshape(db, n_new, h_fox))

    return (yp.reshape(batch, t, d), ys.reshape(db, n_new, d)) + tuple(jnp.stack(o) for o in outs)
```

```python
import functools

import jax
import jax.numpy as jnp
from jax import lax
from jax.experimental import pallas as pl
from jax.experimental.pallas import tpu as pltpu

F32 = jnp.float32
BF16 = jnp.bfloat16

LANES = 128
SUBLANES = 8
VMEM_LIMIT_BYTES = 56 * 1024 * 1024

HEAD_DIM = 128
MOBA_BLOCK = 256
MOBA_TOPK = 3
ROPE_THETA = 10000.0
NORM_EPS = 1e-6
Q_TILE = 256
HEAD_GROUP = SUBLANES

_NT = (((1,), (1,)), ((), ()))
_HI = lax.Precision.HIGHEST


def _cparams(*sem):
    return pltpu.CompilerParams(dimension_semantics=sem, vmem_limit_bytes=VMEM_LIMIT_BYTES)


def _dot(a, b):
    return jnp.dot(a, b, preferred_element_type=F32)


def _dot_nt(a, b):
    return lax.dot_general(a, b, _NT, preferred_element_type=F32)


def _split3(x):
    hi = x.astype(BF16)
    r = x - hi.astype(F32)
    mid = r.astype(BF16)
    lo = (r - mid.astype(F32)).astype(BF16)
    return hi, mid, lo


def _tri_dot_left(tri, x):
    hi, mid, lo = _split3(x)
    return _dot(tri, hi) + _dot(tri, mid) + _dot(tri, lo)


def _tri_dot_right(x, tri):
    hi, mid, lo = _split3(x)
    return _dot(hi, tri) + _dot(mid, tri) + _dot(lo, tri)


def _log_sigmoid_pair(z):
    sp = jnp.log(1.0 + jnp.exp(-jnp.abs(z)))
    return jnp.minimum(z, 0.0) - sp, jnp.minimum(-z, 0.0) - sp


def _tile(m, pref):
    t = pref
    while m % t:
        t //= 2
    assert t >= 1 and (t % SUBLANES == 0 or t == m), (m, pref)
    return min(t, m)


def _rmsnorm_body(x_ref, g_ref, o_ref):
    x = x_ref[...]
    ms = jnp.mean(x * x, axis=-1, keepdims=True)
    o_ref[...] = (x * lax.rsqrt(ms + NORM_EPS) * g_ref[...]).astype(o_ref.dtype)


def _rmsnorm(x, g):
    m, d = x.shape
    tm = _tile(m, 512)
    return pl.pallas_call(
        _rmsnorm_body,
        grid=(m // tm,),
        in_specs=[pl.BlockSpec((tm, d), lambda i: (i, 0)), pl.BlockSpec((1, d), lambda i: (0, 0))],
        out_specs=pl.BlockSpec((tm, d), lambda i: (i, 0)),
        out_shape=jax.ShapeDtypeStruct((m, d), BF16),
        compiler_params=_cparams("parallel"),
        name="rmsnorm",
    )(x, g.reshape(1, d))


def _ffn_up_body(x_ref, wg_ref, wu_ref, o_ref):
    x = x_ref[...]
    g = _dot(x, wg_ref[...])
    u = _dot(x, wu_ref[...])
    o_ref[...] = (g * jax.nn.sigmoid(g) * u).astype(o_ref.dtype)


def _ffn_up(xn, wg, wu):
    m, k = xn.shape
    n = wg.shape[1]
    tm, tn = _tile(m, 1024), _tile(n, 512)
    return pl.pallas_call(
        _ffn_up_body,
        grid=(m // tm, n // tn),
        in_specs=[pl.BlockSpec((tm, k), lambda i, j: (i, 0)),
                  pl.BlockSpec((k, tn), lambda i, j: (0, j)),
                  pl.BlockSpec((k, tn), lambda i, j: (0, j))],
        out_specs=pl.BlockSpec((tm, tn), lambda i, j: (i, j)),
        out_shape=jax.ShapeDtypeStruct((m, n), BF16),
        compiler_params=_cparams("parallel", "arbitrary"),
        name="ffn_up",
    )(xn, wg, wu)


def _mm_resid_body(x_ref, w_ref, r_ref, o_ref, *, scale):
    o_ref[...] = r_ref[...] + scale * _dot(x_ref[...], w_ref[...])


def _mm_resid(x, w, resid, scale):
    m, k = x.shape
    n = w.shape[1]
    tm, tn = _tile(m, 1024), _tile(n, 512)
    return pl.pallas_call(
        functools.partial(_mm_resid_body, scale=scale),
        grid=(m // tm, n // tn),
        in_specs=[pl.BlockSpec((tm, k), lambda i, j: (i, 0)),
                  pl.BlockSpec((k, tn), lambda i, j: (0, j)),
                  pl.BlockSpec((tm, tn), lambda i, j: (i, j))],
        out_specs=pl.BlockSpec((tm, tn), lambda i, j: (i, j)),
        out_shape=jax.ShapeDtypeStruct((m, n), F32),
        compiler_params=_cparams("parallel", "arbitrary"),
        name="mm_resid",
    )(x, w, resid)


PROJ_TN = 2 * HEAD_DIM


def _w_in_spec(w_in, layer, col_of):
    return pl.BlockSpec((None, w_in.shape[1], PROJ_TN), lambda *ids: (layer, 0, col_of(*ids)))


def _proj_qk_body(x_ref, w_ref, g_ref, cos_ref, sin_ref, *out_refs, n_rope, n_plain_end):
    j = pl.program_id(1)
    acc = _dot(x_ref[...], w_ref[...].astype(BF16))

    def write(val):
        out_refs[0][...] = val.astype(BF16)
        if len(out_refs) > 1:
            out_refs[1][...] = val

    def normed(c):
        a = acc[:, c * HEAD_DIM:(c + 1) * HEAD_DIM]
        ms = jnp.mean(a * a, axis=-1, keepdims=True)
        return a * lax.rsqrt(ms + NORM_EPS) * g_ref[0]

    n_heads_tile = PROJ_TN // HEAD_DIM

    @pl.when(j < n_rope)
    def _():
        cos, sin = cos_ref[...], sin_ref[...]
        outs = []
        for c in range(n_heads_tile):
            a = normed(c)
            outs.append(a * cos + pltpu.roll(a, HEAD_DIM // 2, 1) * sin)
        write(jnp.concatenate(outs, axis=-1))

    @pl.when(jnp.logical_and(j >= n_rope, j < n_plain_end))
    def _():
        write(acc)

    @pl.when(j >= n_plain_end)
    def _():
        write(jnp.concatenate([normed(c) for c in range(n_heads_tile)], axis=-1))


def _proj_qk(xn, w_in, layer, col_tile0, gains, cos_t, sin_t, heads, emit_f32):
    m, k = xn.shape
    h_moba, h_sb, h_fox = heads
    w_mix = (h_moba + h_sb + h_fox) * HEAD_DIM
    per = PROJ_TN // HEAD_DIM
    assert h_moba % per == 0 and h_sb % per == 0 and h_fox % per == 0
    n_rope, n_plain_end, n_tiles = h_moba // per, (h_moba + h_sb) // per, w_mix // PROJ_TN
    assert m % cos_t.shape[0] == 0
    tm = _tile(cos_t.shape[0], 1024)
    n_tbl = cos_t.shape[0] // tm
    out_shape = [jax.ShapeDtypeStruct((m, w_mix), BF16)]
    out_specs = [pl.BlockSpec((tm, PROJ_TN), lambda i, j: (i, j))]
    if emit_f32:
        out_shape.append(jax.ShapeDtypeStruct((m, w_mix), F32))
        out_specs.append(pl.BlockSpec((tm, PROJ_TN), lambda i, j: (i, j)))
    return pl.pallas_call(
        functools.partial(_proj_qk_body, n_rope=n_rope, n_plain_end=n_plain_end),
        grid=(m // tm, n_tiles),
        in_specs=[pl.BlockSpec((tm, k), lambda i, j: (i, 0)),
                  _w_in_spec(w_in, layer, lambda i, j: col_tile0 + j),
                  pl.BlockSpec((None, 1, HEAD_DIM), lambda i, j: (jnp.where(j >= n_plain_end, 1, 0), 0, 0)),
                  pl.BlockSpec((tm, HEAD_DIM), lambda i, j: (i % n_tbl, 0)),
                  pl.BlockSpec((tm, HEAD_DIM), lambda i, j: (i % n_tbl, 0))],
        out_specs=out_specs,
        out_shape=out_shape,
        compiler_params=_cparams("parallel", "arbitrary"),
        name="proj_qk",
    )(xn, w_in, gains, cos_t, sin_t)


def _proj_v_body(x_ref, w_ref, o_bf_ref, o_f32_ref):
    acc = _dot(x_ref[...], w_ref[...].astype(BF16))
    o_bf_ref[...] = acc.astype(BF16)
    o_f32_ref[...] = acc


def _proj_v(xn, w_in, layer, col_tile0, n_cols):
    m, k = xn.shape
    tm = _tile(m, 1024)
    spec = pl.BlockSpec((tm, PROJ_TN), lambda i, j: (i, j))
    return pl.pallas_call(
        _proj_v_body,
        grid=(m // tm, n_cols // PROJ_TN),
        in_specs=[pl.BlockSpec((tm, k), lambda i, j: (i, 0)),
                  _w_in_spec(w_in, layer, lambda i, j: col_tile0 + j)],
        out_specs=[spec, spec],
        out_shape=[jax.ShapeDtypeStruct((m, n_cols), BF16), jax.ShapeDtypeStruct((m, n_cols), F32)],
        compiler_params=_cparams("parallel", "arbitrary"),
        name="proj_v",
    )(xn, w_in)


def _proj_gate_body(x_ref, w_ref, o_ref):
    o_ref[...] = jax.nn.sigmoid(_dot(x_ref[...], w_ref[...].astype(BF16)))


def _proj_gate(xn, w_in, layer, col_tile0, n_cols):
    m, k = xn.shape
    tm = _tile(m, 1024)
    return pl.pallas_call(
        _proj_gate_body,
        grid=(m // tm, n_cols // PROJ_TN),
        in_specs=[pl.BlockSpec((tm, k), lambda i, j: (i, 0)),
                  _w_in_spec(w_in, layer, lambda i, j: col_tile0 + j)],
        out_specs=pl.BlockSpec((tm, PROJ_TN), lambda i, j: (i, j)),
        out_shape=jax.ShapeDtypeStruct((m, n_cols), F32),
        compiler_params=_cparams("parallel", "arbitrary"),
        name="proj_gate",
    )(xn, w_in)


def _proj_logf_body(x_ref, w_ref, b_ref, o_ref, *, n_valid):
    acc = _dot(x_ref[...], w_ref[...].astype(BF16))[:, :LANES]
    lane = lax.broadcasted_iota(jnp.int32, acc.shape, 1)
    ls, _ = _log_sigmoid_pair(acc + b_ref[...])
    o_ref[...] = jnp.where(lane < n_valid, ls, 0.0)


def _proj_logf(xn, w_in, layer, col_tile0, b_pad, n_valid):
    m, k = xn.shape
    tm = _tile(m, 1024)
    return pl.pallas_call(
        functools.partial(_proj_logf_body, n_valid=n_valid),
        grid=(m // tm,),
        in_specs=[pl.BlockSpec((tm, k), lambda i: (i, 0)),
                  _w_in_spec(w_in, layer, lambda i: col_tile0),
                  pl.BlockSpec((1, LANES), lambda i: (0, 0))],
        out_specs=pl.BlockSpec((tm, LANES), lambda i: (i, 0)),
        out_shape=jax.ShapeDtypeStruct((m, LANES), F32),
        compiler_params=_cparams("parallel"),
        name="proj_logf",
    )(xn, w_in, b_pad)


def _merge_body(oa_ref, ob_ref, oc_ref, wa_ref, wb_ref, wc_ref, ga_ref, gb_ref, gc_ref, o_ref):
    merged = (ga_ref[...] * _dot(oa_ref[...].astype(BF16), wa_ref[...])
              + gb_ref[...] * _dot(ob_ref[...].astype(BF16), wb_ref[...])
              + gc_ref[...] * _dot(oc_ref[...].astype(BF16), wc_ref[...]))
    o_ref[...] = merged.astype(o_ref.dtype)


def _merge(o_a, o_b, o_c, wb_a, wb_b, wb_c, gates):
    m = o_a.shape[0]
    d = wb_a.shape[1]
    tm, tn = _tile(m, 1024), _tile(d, 512)
    nd = d // tn

    def o_spec(o):
        return pl.BlockSpec((tm, o.shape[1]), lambda i, j: (i, 0))

    def w_spec(w):
        return pl.BlockSpec((w.shape[0], tn), lambda i, j: (0, j))

    def g_spec(b):
        return pl.BlockSpec((tm, tn), lambda i, j: (i, b * nd + j))

    return pl.pallas_call(
        _merge_body,
        grid=(m // tm, nd),
        in_specs=[o_spec(o_a), o_spec(o_b), o_spec(o_c), w_spec(wb_a), w_spec(wb_b), w_spec(wb_c),
                  g_spec(0), g_spec(1), g_spec(2)],
        out_specs=pl.BlockSpec((tm, tn), lambda i, j: (i, j)),
        out_shape=jax.ShapeDtypeStruct((m, d), BF16),
        compiler_params=_cparams("parallel", "arbitrary"),
        name="merge",
    )(o_a, o_b, o_c, wb_a, wb_b, wb_c, gates, gates, gates)


def _rope_table_body(pos_ref, inv_ref, cos_ref, sin_ref):
    ang = pos_ref[...] * inv_ref[...]
    lane = lax.broadcasted_iota(jnp.int32, ang.shape, 1)
    cos_ref[...] = jnp.cos(ang)
    sin_ref[...] = jnp.where(lane < HEAD_DIM // 2, -1.0, 1.0) * jnp.sin(ang)


def _rope_tables(pos):
    half = HEAD_DIM // 2
    inv = ROPE_THETA ** (-jnp.arange(half, dtype=F32) / half)
    inv_full = jnp.concatenate([inv, inv]).reshape(1, HEAD_DIM)
    r = pos.shape[0]
    spec = pl.BlockSpec((r, HEAD_DIM), lambda: (0, 0))
    return pl.pallas_call(
        _rope_table_body,
        in_specs=[pl.BlockSpec((r, 1), lambda: (0, 0)), pl.BlockSpec((1, HEAD_DIM), lambda: (0, 0))],
        out_specs=[spec, spec],
        out_shape=[jax.ShapeDtypeStruct((r, HEAD_DIM), F32)] * 2,
        name="rope_tables",
    )(pos.astype(F32).reshape(r, 1), inv_full)


def _tri(n, kind):
    r = lax.broadcasted_iota(jnp.int32, (n, n), 0)
    c = lax.broadcasted_iota(jnp.int32, (n, n), 1)
    return {"row_ge_col": r >= c, "row_gt_col": r > c}[kind].astype(BF16)


def _cumsum_prompt_body(x_ref, tri_ref, fcol_ref, frow_ref, *, n_chunks):
    tri = tri_ref[...]
    c_len = tri.shape[0]
    carry = jnp.zeros((1, LANES), F32)
    for c in range(n_chunks):
        f = _tri_dot_left(tri, x_ref[c * c_len:(c + 1) * c_len, :]) + carry
        fcol_ref[c * c_len:(c + 1) * c_len, :] = f
        frow_ref[:, c * c_len:(c + 1) * c_len] = f.T
        carry = f[c_len - 1:c_len, :]


def _cumsum_prompt(logf, batch, t):
    c_len = MOBA_BLOCK
    return pl.pallas_call(
        functools.partial(_cumsum_prompt_body, n_chunks=t // c_len),
        grid=(batch,),
        in_specs=[pl.BlockSpec((t, LANES), lambda b: (b, 0)),
                  pl.BlockSpec((c_len, c_len), lambda b: (0, 0))],
        out_specs=[pl.BlockSpec((t, LANES), lambda b: (b, 0)),
                   pl.BlockSpec((None, LANES, t), lambda b: (b, 0, 0))],
        out_shape=[jax.ShapeDtypeStruct((batch * t, LANES), F32),
                   jax.ShapeDtypeStruct((batch, LANES, t), F32)],
        compiler_params=_cparams("parallel"),
        name="cumsum_prompt",
    )(logf, _tri(c_len, "row_ge_col"))


CUMSUM_PAGES_PER_STEP = 8


def _cumsum_pages_body(pt_ref, *refs, n_heads, n_in):
    x_refs = refs[:n_in]
    tri_ref, f_ref, pad_ref, carry_ref = refs[n_in:]
    page = tri_ref.shape[0]

    @pl.when(pl.program_id(1) == 0)
    def _():
        carry_ref[...] = jnp.zeros_like(carry_ref)

    pad_ref[...] = jnp.zeros_like(pad_ref)
    carry = carry_ref[...]
    for e in range(n_in):
        pad_ref[:, :n_heads] = x_refs[e][...]
        f = _tri_dot_left(tri_ref[...], pad_ref[...]) + carry
        f_ref[e * page:(e + 1) * page, :] = f
        carry = f[page - 1:, :]
    carry_ref[...] = carry


def _cumsum_pages(cache_logf, layer, page_table):
    _, _, page, n_heads = cache_logf.shape
    db, n_pages = page_table.shape
    assert page == LANES
    n_in = CUMSUM_PAGES_PER_STEP
    while n_pages % n_in:
        n_in //= 2

    def page_spec(e):
        return pl.BlockSpec((None, None, page, n_heads), lambda b, p, pt: (layer, pt[b, p * n_in + e], 0, 0))

    grid_spec = pltpu.PrefetchScalarGridSpec(
        num_scalar_prefetch=1,
        grid=(db, n_pages // n_in),
        in_specs=[page_spec(e) for e in range(n_in)] + [pl.BlockSpec((page, page), lambda b, p, pt: (0, 0))],
        out_specs=pl.BlockSpec((None, n_in * page, LANES), lambda b, p, pt: (b, p, 0)),
        scratch_shapes=[pltpu.VMEM((page, LANES), F32), pltpu.VMEM((1, LANES), F32)],
    )
    return pl.pallas_call(
        functools.partial(_cumsum_pages_body, n_heads=n_heads, n_in=n_in),
        grid_spec=grid_spec,
        out_shape=jax.ShapeDtypeStruct((db, n_pages * page, LANES), F32),
        compiler_params=_cparams("parallel", "arbitrary"),
        name="cumsum_pages",
    )(page_table, *([cache_logf] * n_in), _tri(page, "row_ge_col"))


HEADS_PER_STEP = 2


def _head_lanes(e):
    return slice(e * HEAD_DIM, (e + 1) * HEAD_DIM)


def _causal_iotas(n):
    row = lax.broadcasted_iota(jnp.int32, (n, n), 0)
    col = lax.broadcasted_iota(jnp.int32, (n, n), 1)
    return row, col


def _moba_select(q, kmean, i, nb):
    shift = max(nb - 1, 1).bit_length()
    width = 1 << shift
    assert width * width <= LANES

    def hi_part(x):
        return lax.shift_right_logical(x, shift)

    def lo_part(x):
        return jnp.bitwise_and(x, width - 1)

    gs = lax.dot_general(q.astype(F32), kmean, _NT, precision=_HI, preferred_element_type=F32)
    r = lax.broadcasted_iota(jnp.int32, (LANES, LANES), 0)
    c = lax.broadcasted_iota(jnp.int32, (LANES, LANES), 1)
    pair_ok = c < width * width
    diff_mat = jnp.where(pair_ok, jnp.where(r == hi_part(c), 1.0, 0.0) - jnp.where(r == lo_part(c), 1.0, 0.0), 0.0)
    diff = jnp.dot(gs, diff_mat, precision=_HI, preferred_element_type=F32)
    lane = lax.broadcasted_iota(jnp.int32, diff.shape, 1)
    n_l, m_l = hi_part(lane), lo_part(lane)
    beats = jnp.logical_or(diff < 0.0, jnp.logical_and(diff == 0.0, m_l < n_l))
    beats = jnp.logical_and(beats, jnp.logical_and(m_l < i, lane < width * width))
    sum_mat = jnp.where(jnp.logical_and(hi_part(r) == c, r < width * width), 1.0, 0.0).astype(BF16)
    rank = _dot(jnp.where(beats, 1.0, 0.0).astype(BF16), sum_mat)
    return jnp.where(jnp.logical_and(rank < MOBA_TOPK, lane < i), 1.0, 0.0)


def _moba_prompt_body(q_ref, k_ref, v_ref, o_ref, kmean_ref, *, nb, scale):
    i = pl.program_id(2)
    blk = MOBA_BLOCK

    @pl.when(i == 0)
    def _():
        kmean_ref[...] = jnp.zeros_like(kmean_ref)
        for e in range(HEADS_PER_STEP):
            for n in range(nb):
                kmean_ref[e, n:n + 1, :] = jnp.mean(k_ref[n * blk:(n + 1) * blk, _head_lanes(e)].astype(F32),
                                                    axis=0, keepdims=True)

    start = pl.multiple_of(i * blk, blk)
    row, col = _causal_iotas(blk)
    causal = col <= row
    qs = [q_ref[:, _head_lanes(e)] for e in range(HEADS_PER_STEP)]
    sels = [_moba_select(qs[e], kmean_ref[e], i, nb) for e in range(HEADS_PER_STEP)]
    lane = lax.broadcasted_iota(jnp.int32, sels[0].shape, 1)

    def own(e):
        s = jnp.where(causal, _dot_nt(qs[e], k_ref[pl.ds(start, blk), _head_lanes(e)]) * scale, -jnp.inf)
        m0 = jnp.max(s, axis=-1, keepdims=True)
        p = jnp.exp(s - m0)
        return m0, jnp.sum(p, axis=-1, keepdims=True), _dot(p.astype(BF16), v_ref[pl.ds(start, blk), _head_lanes(e)])

    def past(n, carry):
        st = pl.multiple_of(n * blk, blk)
        out = []
        for e in range(HEADS_PER_STEP):
            m, l, acc = carry[e]
            chosen = jnp.sum(jnp.where(lane == n, sels[e], 0.0), axis=-1, keepdims=True) > 0.5
            s = jnp.where(chosen, _dot_nt(qs[e], k_ref[pl.ds(st, blk), _head_lanes(e)]) * scale, -jnp.inf)
            m_new = jnp.maximum(m, jnp.max(s, axis=-1, keepdims=True))
            alpha = jnp.exp(m - m_new)
            p = jnp.exp(s - m_new)
            l = alpha * l + jnp.sum(p, axis=-1, keepdims=True)
            acc = alpha * acc + _dot(p.astype(BF16), v_ref[pl.ds(st, blk), _head_lanes(e)])
            out.append((m_new, l, acc))
        return tuple(out)

    fin = lax.fori_loop(0, i, past, tuple(own(e) for e in range(HEADS_PER_STEP)))
    o_ref[...] = jnp.concatenate([acc / l for _, l, acc in fin], axis=-1).astype(o_ref.dtype)


def _sb_prompt_body(q_ref, k_ref, v_ref, tri_ref, o_ref, *, scale):
    i = pl.program_id(2)
    blk = Q_TILE
    qs = [q_ref[:, _head_lanes(e)] for e in range(HEADS_PER_STEP)]
    tri = tri_ref[...]

    def block(e, st, carry, acc, strict):
        z = _dot_nt(qs[e], k_ref[pl.ds(st, blk), _head_lanes(e)]) * scale
        ls, lsn = _log_sigmoid_pair(z)
        if strict is not None:
            lsn = jnp.where(strict, lsn, 0.0)
        hi = lsn.astype(BF16)
        lo = (lsn - hi.astype(F32)).astype(BF16)
        later = _dot(hi, tri) + _dot(lo, tri)
        w = jnp.exp(ls + later + carry)
        if strict is not None:
            w = jnp.where(strict, w, 0.0)
        acc = acc + _dot(w.astype(BF16), v_ref[pl.ds(st, blk), _head_lanes(e)])
        return carry + jnp.sum(lsn, axis=-1, keepdims=True), acc

    row, col = _causal_iotas(blk)
    init = tuple(block(e, pl.multiple_of(i * blk, blk), jnp.zeros((blk, 1), F32),
                       jnp.zeros((blk, HEAD_DIM), F32), col < row) for e in range(HEADS_PER_STEP))

    def past(t, c):
        st = pl.multiple_of((i - 1 - t) * blk, blk)
        return tuple(block(e, st, c[e][0], c[e][1], None) for e in range(HEADS_PER_STEP))

    fin = lax.fori_loop(0, i, past, init)
    o_ref[...] = jnp.concatenate([acc for _, acc in fin], axis=-1).astype(o_ref.dtype)


def _fox_prompt_body(q_ref, k_ref, v_ref, fcol_ref, frow_ref, o_ref, *, scale):
    h = pl.program_id(1)
    i = pl.program_id(2)
    blk = Q_TILE
    qs = [q_ref[:, _head_lanes(e)] for e in range(HEADS_PER_STEP)]
    fcol = fcol_ref[...]
    lane = lax.broadcasted_iota(jnp.int32, fcol.shape, 1)
    fqs = [jnp.sum(jnp.where(lane == h * HEADS_PER_STEP + e, fcol, 0.0), axis=-1, keepdims=True)
           for e in range(HEADS_PER_STEP)]

    def scores(e, st):
        fk = frow_ref[pl.ds(h * HEADS_PER_STEP + e, 1), pl.ds(st, blk)]
        return _dot_nt(qs[e], k_ref[pl.ds(st, blk), _head_lanes(e)]) * scale + (fqs[e] - fk)

    start = pl.multiple_of(i * blk, blk)
    row, col = _causal_iotas(blk)
    causal = col <= row

    def own(e):
        s = jnp.where(causal, scores(e, start), -jnp.inf)
        m0 = jnp.max(s, axis=-1, keepdims=True)
        p = jnp.exp(s - m0)
        return m0, jnp.sum(p, axis=-1, keepdims=True), _dot(p.astype(BF16), v_ref[pl.ds(start, blk), _head_lanes(e)])

    def past(n, carry):
        st = pl.multiple_of(n * blk, blk)
        out = []
        for e in range(HEADS_PER_STEP):
            m, l, acc = carry[e]
            s = scores(e, st)
            m_new = jnp.maximum(m, jnp.max(s, axis=-1, keepdims=True))
            alpha = jnp.exp(m - m_new)
            p = jnp.exp(s - m_new)
            l = alpha * l + jnp.sum(p, axis=-1, keepdims=True)
            acc = alpha * acc + _dot(p.astype(BF16), v_ref[pl.ds(st, blk), _head_lanes(e)])
            out.append((m_new, l, acc))
        return tuple(out)

    fin = lax.fori_loop(0, i, past, tuple(own(e) for e in range(HEADS_PER_STEP)))
    o_ref[...] = jnp.concatenate([acc / l for _, l, acc in fin], axis=-1).astype(o_ref.dtype)


def _prompt_attention(q_bf, k_bf, v_bf, fcol, frow, batch, t, heads):
    h_moba, h_sb, h_fox = heads
    assert t % Q_TILE == 0 and Q_TILE == MOBA_BLOCK
    nq = t // Q_TILE
    m = batch * t
    scale = HEAD_DIM ** -0.5
    hps = HEADS_PER_STEP
    assert h_moba % hps == 0 and h_sb % hps == 0 and h_fox % hps == 0
    width = hps * HEAD_DIM

    def specs(h0):
        qs = pl.BlockSpec((Q_TILE, width), lambda b, h, i: (b * nq + i, h0 // hps + h))
        ks = pl.BlockSpec((t, width), lambda b, h, i: (b, h0 // hps + h))
        return qs, ks, ks

    def out(n_heads):
        return (pl.BlockSpec((Q_TILE, width), lambda b, h, i: (b * nq + i, h)),
                jax.ShapeDtypeStruct((m, n_heads * HEAD_DIM), BF16))

    sem = _cparams("parallel", "parallel", "arbitrary")

    o_spec, o_shape = out(h_moba)
    o_a = pl.pallas_call(
        functools.partial(_moba_prompt_body, nb=nq, scale=scale),
        grid=(batch, h_moba // hps, nq),
        in_specs=list(specs(0)),
        out_specs=o_spec, out_shape=o_shape,
        scratch_shapes=[pltpu.VMEM((hps, LANES, HEAD_DIM), F32)],
        compiler_params=sem, name="moba_prompt",
    )(q_bf, k_bf, v_bf)

    o_spec, o_shape = out(h_sb)
    o_b = pl.pallas_call(
        functools.partial(_sb_prompt_body, scale=scale),
        grid=(batch, h_sb // hps, nq),
        in_specs=list(specs(h_moba)) + [pl.BlockSpec((Q_TILE, Q_TILE), lambda b, h, i: (0, 0))],
        out_specs=o_spec, out_shape=o_shape,
        compiler_params=sem, name="sb_prompt",
    )(q_bf, k_bf, v_bf, _tri(Q_TILE, "row_gt_col"))

    o_spec, o_shape = out(h_fox)
    o_c = pl.pallas_call(
        functools.partial(_fox_prompt_body, scale=scale),
        grid=(batch, h_fox // hps, nq),
        in_specs=list(specs(h_moba + h_sb)) + [
            pl.BlockSpec((Q_TILE, LANES), lambda b, h, i: (b * nq + i, 0)),
            pl.BlockSpec((None, LANES, t), lambda b, h, i: (b, 0, 0))],
        out_specs=o_spec, out_shape=o_shape,
        compiler_params=sem, name="fox_prompt",
    )(q_bf, k_bf, v_bf, fcol, frow)
    return o_a, o_b, o_c


def _sample_attn_body(pt_ref, q_ref, kn_ref, vn_ref, lfn_ref, fexp_ref, ftot_ref, tri_ref, utile_ref, *refs,
                      heads, n_groups, n_pages, scale, n_new):
    h_moba, h_sb, h_fox = heads
    h_tot = h_moba + h_sb + h_fox
    k_refs = refs[:n_groups]
    v_refs = refs[n_groups:2 * n_groups]
    o_ref = refs[2 * n_groups]
    (kpad_ref, vpad_ref, acc_ref, m_ref, l_ref, carry_ref, fq_ref, cumn_t_ref,
     pm_ref, pl_ref, po_ref, pg_ref, gsb_ref) = refs[2 * n_groups + 1:]

    r = pl.program_id(1)
    page = LANES
    rows = q_ref.shape[0]
    pages_per_block = MOBA_BLOCK // page
    grp_rows = HEAD_GROUP * rows
    grp_keys = HEAD_GROUP * page

    def q_head(h):
        return q_ref[:, h * HEAD_DIM:(h + 1) * HEAD_DIM].astype(BF16)

    def bcast(x):
        return jnp.broadcast_to(x, (x.shape[0], LANES))

    def head_rows(h):
        return slice(h * rows, (h + 1) * rows)

    @pl.when(r == 0)
    def _():
        kpad_ref[...] = jnp.zeros_like(kpad_ref)
        vpad_ref[...] = jnp.zeros_like(vpad_ref)
        kpad_ref[0:n_new, :] = kn_ref[...]
        vpad_ref[0:n_new, :] = vn_ref[...]
        row = lax.broadcasted_iota(jnp.int32, (rows, page), 0)
        col = lax.broadcasted_iota(jnp.int32, (rows, page), 1)
        weak = col <= row
        strict = col < row

        cumn_t_ref[...] = jnp.zeros_like(cumn_t_ref)
        cumn_t_ref[0:rows, :] = lfn_ref[...]
        lfn_pad = cumn_t_ref[...]
        cum_pad = _tri_dot_left(tri_ref[...], lfn_pad) + lfn_pad
        cumn = cum_pad[0:rows, :]
        cumn_t_ref[...] = cum_pad.T

        for h in range(h_tot):
            hr = head_rows(h)
            kh = kpad_ref[:, h * HEAD_DIM:(h + 1) * HEAD_DIM].astype(BF16)
            vh = vpad_ref[:, h * HEAD_DIM:(h + 1) * HEAD_DIM].astype(BF16)
            s_raw = _dot_nt(q_head(h), kh)
            if h < h_moba:
                s = jnp.where(weak, s_raw * scale, -jnp.inf)
                m = jnp.max(s, axis=-1, keepdims=True)
                p = jnp.exp(s - m)
                pm_ref[n_pages, hr] = bcast(m)
                pl_ref[n_pages, hr] = bcast(jnp.sum(p, axis=-1, keepdims=True))
                po_ref[n_pages, hr] = _dot(p.astype(BF16), vh)
            elif h < h_moba + h_sb:
                ls, lsn = _log_sigmoid_pair(s_raw * scale)
                lsn = jnp.where(strict, lsn, 0.0)
                later = _tri_dot_right(lsn, tri_ref[...])
                w = jnp.where(strict, jnp.exp(ls + later), 0.0)
                acc_ref[hr] = _dot(w.astype(BF16), vh)
                carry_ref[hr] = bcast(jnp.sum(lsn, axis=-1, keepdims=True))
            else:
                c = h - h_moba - h_sb
                bias = cumn[:, c:c + 1] - cumn_t_ref[c:c + 1, :]
                s = jnp.where(weak, s_raw * scale + bias, -jnp.inf)
                m = jnp.max(s, axis=-1, keepdims=True)
                p = jnp.exp(s - m)
                m_ref[hr] = bcast(m)
                l_ref[hr] = bcast(jnp.sum(p, axis=-1, keepdims=True))
                acc_ref[hr] = _dot(p.astype(BF16), vh)
                fq_ref[hr] = bcast(cumn[:, c:c + 1] + ftot_ref[0:1, c:c + 1])

    lane = lax.broadcasted_iota(jnp.int32, (grp_rows, grp_keys), 1)
    qrow = lax.broadcasted_iota(jnp.int32, (grp_rows, grp_keys), 0)
    own = jnp.bitwise_and(lane, HEAD_GROUP - 1) == lax.shift_right_logical(qrow, 3)
    assert rows == 8 and HEAD_GROUP == 8
    n_tiles = grp_keys // LANES

    for g in range(n_groups):
        h0 = g * HEAD_GROUP
        n_m = min(max(h_moba - h0, 0), HEAD_GROUP)
        n_s = min(max(h_moba + h_sb - h0, 0), HEAD_GROUP) - n_m
        r_m, r_s = n_m * rows, (n_m + n_s) * rows
        base = h0 * rows

        qg = jnp.concatenate([q_ref[:, (h0 + hh) * HEAD_DIM:(h0 + hh + 1) * HEAD_DIM]
                              for hh in range(HEAD_GROUP)], axis=0).astype(BF16)
        kb = k_refs[g][...].reshape(grp_keys, HEAD_DIM).astype(BF16)
        vb = v_refs[g][...].reshape(grp_keys, HEAD_DIM).astype(BF16)
        raw = _dot_nt(qg, kb)
        probs = []

        if r_m > 0:
            sl = slice(base, base + r_m)
            raw_m, own_m = raw[0:r_m], own[0:r_m]
            s = jnp.where(own_m, raw_m * scale, -jnp.inf)
            m = jnp.max(s, axis=-1, keepdims=True)
            p = jnp.exp(s - m)
            pm_ref[r, sl] = bcast(m)
            pl_ref[r, sl] = bcast(jnp.sum(p, axis=-1, keepdims=True))
            pg_ref[r, sl] = bcast(jnp.sum(jnp.where(own_m, raw_m, 0.0), axis=-1, keepdims=True))
            probs.append(p)

        if r_s > r_m:
            sl = slice(base + r_m, base + r_s)
            n_r = r_s - r_m
            own_s = own[r_m:r_s]
            ls, lsn = _log_sigmoid_pair(raw[r_m:r_s] * scale)
            lsn = jnp.where(own_s, lsn, 0.0)
            stacked = jnp.concatenate([lsn[:, t * LANES:(t + 1) * LANES] for t in range(n_tiles)], axis=0)
            hi = stacked.astype(BF16)
            lo = (stacked - hi.astype(F32)).astype(BF16)
            intra = _dot(hi, utile_ref[...]) + _dot(lo, utile_ref[...])
            carry = carry_ref[sl][:, 0:1]
            suffix = carry
            later = [None] * n_tiles
            for t in reversed(range(n_tiles)):
                later[t] = intra[t * n_r:(t + 1) * n_r] + suffix
                suffix = suffix + jnp.sum(lsn[:, t * LANES:(t + 1) * LANES], axis=-1, keepdims=True)
            carry_ref[sl] = bcast(suffix)
            w = jnp.where(own_s, jnp.exp(ls + jnp.concatenate(later, axis=-1)), 0.0)
            probs.append(w)

        if r_s < grp_rows:
            sl = slice(base + r_s, base + grp_rows)
            own_f = own[r_s:]
            bias = fq_ref[sl][:, 0:1] - fexp_ref[g:g + 1, :]
            s = jnp.where(own_f, raw[r_s:] * scale + bias, -jnp.inf)
            m_old = m_ref[sl][:, 0:1]
            m_new = jnp.maximum(m_old, jnp.max(s, axis=-1, keepdims=True))
            alpha = jnp.exp(m_old - m_new)
            p = jnp.exp(s - m_new)
            l_ref[sl] = alpha * l_ref[sl] + bcast(jnp.sum(p, axis=-1, keepdims=True))
            m_ref[sl] = bcast(m_new)
            probs.append(p)

        out = _dot(jnp.concatenate(probs, axis=0).astype(BF16), vb)
        if r_m > 0:
            po_ref[r, base:base + r_m] = out[0:r_m]
        if r_s > r_m:
            sl = slice(base + r_m, base + r_s)
            acc_ref[sl] = acc_ref[sl] + out[r_m:r_s]
        if r_s < grp_rows:
            sl = slice(base + r_s, base + grp_rows)
            acc_ref[sl] = alpha * acc_ref[sl] + out[r_s:]

    @pl.when(r == n_pages - 1)
    def _():
        n_blocks = n_pages // pages_per_block
        moba_rows = h_moba * rows

        def fill(n, _):
            tot = pg_ref[n_pages - 1 - n * pages_per_block]
            for e in range(1, pages_per_block):
                tot = tot + pg_ref[n_pages - 1 - (n * pages_per_block + e)]
            gsb_ref[n] = tot * (1.0 / MOBA_BLOCK)
            return 0

        lax.fori_loop(0, n_blocks, fill, 0)

        def add_block(n, carry):
            m_run, l_run, o_run = carry
            gs_n = gsb_ref[n]

            def count(mb, rank):
                gs_m = gsb_ref[mb]
                beats = jnp.logical_or(gs_m > gs_n, jnp.logical_and(gs_m == gs_n, mb < n))
                return rank + jnp.where(beats, 1.0, 0.0)

            rank = lax.fori_loop(0, n_blocks, count, jnp.zeros((moba_rows, LANES), F32))
            chosen = rank < MOBA_TOPK
            for e in range(pages_per_block):
                slot = n_pages - 1 - (n * pages_per_block + e)
                m_p = jnp.where(chosen, pm_ref[slot], -jnp.inf)
                m_new = jnp.maximum(m_run, m_p)
                a_run = jnp.exp(m_run - m_new)
                a_p = jnp.exp(m_p - m_new)
                l_run = a_run * l_run + a_p * pl_ref[slot]
                o_run = a_run * o_run + a_p * po_ref[slot]
                m_run = m_new
            return m_run, l_run, o_run

        _, l_run, o_run = lax.fori_loop(0, n_blocks, add_block,
                                        (pm_ref[n_pages], pl_ref[n_pages], po_ref[n_pages]))
        moba_out = o_run / l_run
        for h in range(h_tot):
            hr = head_rows(h)
            if h < h_moba:
                res = moba_out[hr]
            elif h < h_moba + h_sb:
                res = acc_ref[hr]
            else:
                res = acc_ref[hr] / l_ref[hr]
            o_ref[:, h * HEAD_DIM:(h + 1) * HEAD_DIM] = res


def _sample_attention(q_s, k_new, v_new, logf_new, fcol_past, cache_k, cache_v, layer, page_table, heads):
    h_moba, h_sb, h_fox = heads
    h_tot = h_moba + h_sb + h_fox
    db, n_pages = page_table.shape
    n_new = q_s.shape[0] // db
    _, n_pool, page, hh, dd = cache_k.shape
    assert (page, hh, dd) == (LANES, h_tot, HEAD_DIM) and h_tot % HEAD_GROUP == 0
    assert n_new == SUBLANES and (n_pages * page) % MOBA_BLOCK == 0 and MOBA_BLOCK % page == 0
    n_groups = h_tot // HEAD_GROUP
    w = h_tot * HEAD_DIM
    n_rows = h_tot * n_new
    grp_keys = HEAD_GROUP * page

    zero_lane = LANES - 1
    assert h_fox < LANES
    lane_of_head = [h - h_moba - h_sb if h >= h_moba + h_sb else zero_lane for h in range(h_tot)]
    fexp = fcol_past[:, :, jnp.array(lane_of_head, jnp.int32)]
    fexp = fexp.reshape(db, n_pages, page, n_groups, HEAD_GROUP).transpose(0, 1, 3, 2, 4)
    fexp = fexp.reshape(db, n_pages, n_groups, grp_keys)
    ftot = fcol_past[:, n_pages * page - 1:, :]

    lane = jnp.arange(LANES, dtype=jnp.int32)
    utile = (lane[:, None] // HEAD_GROUP > lane[None, :] // HEAD_GROUP).astype(BF16)

    def row_spec(width):
        return pl.BlockSpec((n_new, width), lambda b, r, pt: (b, 0))

    def const_spec():
        return pl.BlockSpec((LANES, LANES), lambda b, r, pt: (0, 0))

    def cache_spec(g):
        return pl.BlockSpec((None, None, page, HEAD_GROUP, HEAD_DIM),
                            lambda b, r, pt: (layer, pt[b, n_pages - 1 - r], 0, g, 0))

    grid_spec = pltpu.PrefetchScalarGridSpec(
        num_scalar_prefetch=1,
        grid=(db, n_pages),
        in_specs=[row_spec(w), row_spec(w), row_spec(w), row_spec(LANES),
                  pl.BlockSpec((None, None, n_groups, grp_keys), lambda b, r, pt: (b, n_pages - 1 - r, 0, 0)),
                  pl.BlockSpec((None, 1, LANES), lambda b, r, pt: (b, 0, 0)),
                  const_spec(), const_spec()]
                 + [cache_spec(g) for g in range(n_groups)] * 2,
        out_specs=row_spec(w),
        scratch_shapes=[
            pltpu.VMEM((page, w), F32), pltpu.VMEM((page, w), F32),
            pltpu.VMEM((n_rows, HEAD_DIM), F32),
            pltpu.VMEM((n_rows, LANES), F32),
            pltpu.VMEM((n_rows, LANES), F32),
            pltpu.VMEM((n_rows, LANES), F32),
            pltpu.VMEM((n_rows, LANES), F32),
            pltpu.VMEM((LANES, LANES), F32),
            pltpu.VMEM((n_pages + 1, h_moba * n_new, LANES), F32),
            pltpu.VMEM((n_pages + 1, h_moba * n_new, LANES), F32),
            pltpu.VMEM((n_pages + 1, h_moba * n_new, HEAD_DIM), F32),
            pltpu.VMEM((n_pages, h_moba * n_new, LANES), F32),
            pltpu.VMEM((n_pages // (MOBA_BLOCK // page), h_moba * n_new, LANES), F32),
        ],
    )
    body = functools.partial(_sample_attn_body, heads=heads, n_groups=n_groups, n_pages=n_pages,
                             scale=HEAD_DIM ** -0.5, n_new=n_new)
    return pl.pallas_call(
        body,
        grid_spec=grid_spec,
        out_shape=jax.ShapeDtypeStruct((db * n_new, w), F32),
        compiler_params=_cparams("parallel", "arbitrary"),
        name="sample_attn",
    )(page_table, q_s, k_new, v_new, logf_new, fexp, ftot, _tri(page, "row_gt_col"), utile,
      *([cache_k] * n_groups), *([cache_v] * n_groups))


def _layer_common(x, w, heads, cos_t, sin_t):
    h = _mm_resid(_ffn_up(_rmsnorm(x, w["g_f1"]), w["f1_gate"], w["f1_up"]), w["f1_down"], x, 0.5)
    n = _rmsnorm(h, w["g_mix"])
    w_mix = sum(heads) * HEAD_DIM
    d = x.shape[1]
    t0 = w_mix // PROJ_TN
    w_in, layer = w["w_in"], w["layer"]
    q_bf, = _proj_qk(n, w_in, layer, 0, w["q_gains"], cos_t, sin_t, heads, False)
    k_bf, k_f32 = _proj_qk(n, w_in, layer, t0, w["k_gains"], cos_t, sin_t, heads, True)
    v_bf, v_f32 = _proj_v(n, w_in, layer, 2 * t0, w_mix)
    gates = _proj_gate(n, w_in, layer, 3 * t0, 3 * d)
    logf = _proj_logf(n, w_in, layer, 3 * t0 + 3 * d // PROJ_TN, w["b_f"], heads[2])
    return h, q_bf, k_bf, k_f32, v_bf, v_f32, gates, logf


def _layer_tail(h, o_a, o_b, o_c, gates, w):
    merged = _merge(o_a, o_b, o_c, w["wb_a"], w["wb_b"], w["wb_c"], gates)
    h = _mm_resid(merged, w["w_out"], h, 1.0)
    return _mm_resid(_ffn_up(_rmsnorm(h, w["g_f2"]), w["f2_gate"], w["f2_up"]), w["f2_down"], h, 0.5)


def kernel(x_prompt, x_sample, cache_k, cache_v, cache_logf, page_table, norm_ffn1, ffn1_w_gate, ffn1_w_up, ffn1_w_down, norm_mix, w_in, b_forget, q_norm_moba, k_norm_moba, q_norm_fox, k_norm_fox, w_branch_moba, w_branch_sb, w_branch_fox, w_out, norm_ffn2, ffn2_w_gate, ffn2_w_up, ffn2_w_down):
    batch, t, d = x_prompt.shape
    db, n_new, _ = x_sample.shape
    depth = norm_ffn1.shape[0]
    h_fox = b_forget.shape[1]
    h_moba = w_branch_moba.shape[1] // HEAD_DIM
    h_sb = w_branch_sb.shape[1] // HEAD_DIM
    heads = (h_moba, h_sb, h_fox)
    h_tot = sum(heads)
    w_mix = h_tot * HEAD_DIM
    past_len = page_table.shape[1] * cache_k.shape[2]
    assert past_len % MOBA_BLOCK == 0 and n_new <= MOBA_BLOCK
    assert (3 * w_mix + 3 * d) % PROJ_TN == 0 and h_fox <= LANES

    d_ff = ffn1_w_gate.shape[2]
    ff_pad = -d_ff % 512

    def up(wt):
        return jnp.pad(wt.astype(BF16), ((0, 0), (0, 0), (0, ff_pad)))

    def down(wt):
        return jnp.pad(wt.astype(BF16), ((0, 0), (0, ff_pad), (0, 0)))

    f1g, f1u, f1d = up(ffn1_w_gate), up(ffn1_w_up), down(ffn1_w_down)
    f2g, f2u, f2d = up(ffn2_w_gate), up(ffn2_w_up), down(ffn2_w_down)
    wba, wbb, wbc, wo = (a.astype(BF16) for a in (w_branch_moba, w_branch_sb, w_branch_fox, w_out))
    b_pad = jnp.pad(b_forget, ((0, 0), (0, LANES - h_fox)))

    cos_p, sin_p = _rope_tables(jnp.arange(t, dtype=jnp.int32))
    cos_s, sin_s = _rope_tables(past_len + jnp.arange(n_new, dtype=jnp.int32))
    cos_s, sin_s = jnp.tile(cos_s, (db, 1)), jnp.tile(sin_s, (db, 1))

    yp = x_prompt.reshape(batch * t, d)
    ys = x_sample.reshape(db * n_new, d)
    outs = [[] for _ in range(6)]
    for l in range(depth):
        w = dict(g_f1=norm_ffn1[l], f1_gate=f1g[l], f1_up=f1u[l], f1_down=f1d[l], g_mix=norm_mix[l],
                 w_in=w_in, layer=l, b_f=b_pad[l:l + 1],
                 q_gains=jnp.stack([q_norm_moba[l], q_norm_fox[l]]).reshape(2, 1, HEAD_DIM),
                 k_gains=jnp.stack([k_norm_moba[l], k_norm_fox[l]]).reshape(2, 1, HEAD_DIM),
                 wb_a=wba[l], wb_b=wbb[l], wb_c=wbc[l], w_out=wo[l],
                 g_f2=norm_ffn2[l], f2_gate=f2g[l], f2_up=f2u[l], f2_down=f2d[l])

        h, q_bf, k_bf, k_f32, v_bf, v_f32, gates, logf = _layer_common(yp, w, heads, cos_p, sin_p)
        fcol, frow = _cumsum_prompt(logf, batch, t)
        o_a, o_b, o_c = _prompt_attention(q_bf, k_bf, v_bf, fcol, frow, batch, t, heads)
        yp = _layer_tail(h, o_a, o_b, o_c, gates, w)
        outs[0].append(k_f32.reshape(batch, t, h_tot, HEAD_DIM))
        outs[1].append(v_f32.reshape(batch, t, h_tot, HEAD_DIM))
        outs[2].append(logf[:, :h_fox].reshape(batch, t, h_fox))

        h, q_bf, k_bf, k_f32, v_bf, v_f32, gates, logf = _layer_common(ys, w, heads, cos_s, sin_s)
        fcol_past = _cumsum_pages(cache_logf, l, page_table)
        o = _sample_attention(q_bf.astype(F32), k_f32, v_f32, logf, fcol_past, cache_k, cache_v, l,
                              page_table, heads)
        a_end, b_end = h_moba * HEAD_DIM, (h_moba + h_sb) * HEAD_DIM
        ys = _layer_tail(h, o[:, :a_end], o[:, a_end:b_end], o[:, b_end:], gates, w)
        outs[3].append(k_f32.reshape(db, n_new, h_tot, HEAD_DIM))
        outs[4].append(v_f32.reshape(db, n_new, h_tot, HEAD_DIM))
        outs[5].append(logf[:, :h_fox].reshape(db, n_new, h_fox))

    return (yp.reshape(batch, t, d), ys.reshape(db, n_new, d)) + tuple(jnp.stack(o) for o in outs)
```

```python
import functools

import jax
import jax.numpy as jnp
from jax import lax
from jax.experimental import pallas as pl
from jax.experimental.pallas import tpu as pltpu

F32 = jnp.float32
BF16 = jnp.bfloat16

LANES = 128
SUBLANES = 8
VMEM_LIMIT_BYTES = 56 * 1024 * 1024

HEAD_DIM = 128
MOBA_BLOCK = 256
MOBA_TOPK = 3
ROPE_THETA = 10000.0
NORM_EPS = 1e-6
Q_TILE = 256
HEAD_GROUP = SUBLANES

_NT = (((1,), (1,)), ((), ()))
_HI = lax.Precision.HIGHEST


def _cparams(*sem):
    return pltpu.CompilerParams(dimension_semantics=sem, vmem_limit_bytes=VMEM_LIMIT_BYTES)


def _dot(a, b):
    return jnp.dot(a, b, preferred_element_type=F32)


def _dot_nt(a, b):
    return lax.dot_general(a, b, _NT, preferred_element_type=F32)


def _split3(x):
    hi = x.astype(BF16)
    r = x - hi.astype(F32)
    mid = r.astype(BF16)
    lo = (r - mid.astype(F32)).astype(BF16)
    return hi, mid, lo


def _tri_dot_left(tri, x):
    hi, mid, lo = _split3(x)
    return _dot(tri, hi) + _dot(tri, mid) + _dot(tri, lo)


def _tri_dot_right(x, tri):
    hi, mid, lo = _split3(x)
    return _dot(hi, tri) + _dot(mid, tri) + _dot(lo, tri)


def _log_sigmoid_pair(z):
    sp = jnp.log(1.0 + jnp.exp(-jnp.abs(z)))
    return jnp.minimum(z, 0.0) - sp, jnp.minimum(-z, 0.0) - sp


def _tile(m, pref):
    t = pref
    while m % t:
        t //= 2
    assert t >= 1 and (t % SUBLANES == 0 or t == m), (m, pref)
    return min(t, m)


def _rmsnorm_body(x_ref, g_ref, o_ref):
    x = x_ref[...]
    ms = jnp.mean(x * x, axis=-1, keepdims=True)
    o_ref[...] = (x * lax.rsqrt(ms + NORM_EPS) * g_ref[...]).astype(o_ref.dtype)


def _rmsnorm(x, g):
    m, d = x.shape
    tm = _tile(m, 512)
    return pl.pallas_call(
        _rmsnorm_body,
        grid=(m // tm,),
        in_specs=[pl.BlockSpec((tm, d), lambda i: (i, 0)), pl.BlockSpec((1, d), lambda i: (0, 0))],
        out_specs=pl.BlockSpec((tm, d), lambda i: (i, 0)),
        out_shape=jax.ShapeDtypeStruct((m, d), BF16),
        compiler_params=_cparams("parallel"),
        name="rmsnorm",
    )(x, g.reshape(1, d))


def _ffn_up_body(x_ref, wg_ref, wu_ref, o_ref):
    x = x_ref[...]
    g = _dot(x, wg_ref[...])
    u = _dot(x, wu_ref[...])
    o_ref[...] = (g * jax.nn.sigmoid(g) * u).astype(o_ref.dtype)


def _ffn_up(xn, wg, wu):
    m, k = xn.shape
    n = wg.shape[1]
    tm, tn = _tile(m, 1024), _tile(n, 512)
    return pl.pallas_call(
        _ffn_up_body,
        grid=(m // tm, n // tn),
        in_specs=[pl.BlockSpec((tm, k), lambda i, j: (i, 0)),
                  pl.BlockSpec((k, tn), lambda i, j: (0, j)),
                  pl.BlockSpec((k, tn), lambda i, j: (0, j))],
        out_specs=pl.BlockSpec((tm, tn), lambda i, j: (i, j)),
        out_shape=jax.ShapeDtypeStruct((m, n), BF16),
        compiler_params=_cparams("parallel", "arbitrary"),
        name="ffn_up",
    )(xn, wg, wu)


def _mm_resid_body(x_ref, w_ref, r_ref, o_ref, *, scale):
    o_ref[...] = r_ref[...] + scale * _dot(x_ref[...], w_ref[...])


def _mm_resid(x, w, resid, scale):
    m, k = x.shape
    n = w.shape[1]
    tm, tn = _tile(m, 1024), _tile(n, 512)
    return pl.pallas_call(
        functools.partial(_mm_resid_body, scale=scale),
        grid=(m // tm, n // tn),
        in_specs=[pl.BlockSpec((tm, k), lambda i, j: (i, 0)),
                  pl.BlockSpec((k, tn), lambda i, j: (0, j)),
                  pl.BlockSpec((tm, tn), lambda i, j: (i, j))],
        out_specs=pl.BlockSpec((tm, tn), lambda i, j: (i, j)),
        out_shape=jax.ShapeDtypeStruct((m, n), F32),
        compiler_params=_cparams("parallel", "arbitrary"),
        name="mm_resid",
    )(x, w, resid)


PROJ_TN = 2 * HEAD_DIM


def _w_in_spec(w_in, layer, col_of):
    return pl.BlockSpec((None, w_in.shape[1], PROJ_TN), lambda *ids: (layer, 0, col_of(*ids)))


def _proj_qk_body(x_ref, w_ref, g_ref, cos_ref, sin_ref, *out_refs, n_rope, n_plain_end):
    j = pl.program_id(1)
    acc = _dot(x_ref[...], w_ref[...])

    def write(val):
        out_refs[0][...] = val.astype(BF16)
        if len(out_refs) > 1:
            out_refs[1][...] = val

    def normed(c):
        a = acc[:, c * HEAD_DIM:(c + 1) * HEAD_DIM]
        ms = jnp.mean(a * a, axis=-1, keepdims=True)
        return a * lax.rsqrt(ms + NORM_EPS) * g_ref[0]

    n_heads_tile = PROJ_TN // HEAD_DIM

    @pl.when(j < n_rope)
    def _():
        cos, sin = cos_ref[...], sin_ref[...]
        outs = []
        for c in range(n_heads_tile):
            a = normed(c)
            outs.append(a * cos + pltpu.roll(a, HEAD_DIM // 2, 1) * sin)
        write(jnp.concatenate(outs, axis=-1))

    @pl.when(jnp.logical_and(j >= n_rope, j < n_plain_end))
    def _():
        write(acc)

    @pl.when(j >= n_plain_end)
    def _():
        write(jnp.concatenate([normed(c) for c in range(n_heads_tile)], axis=-1))


def _proj_qk(xn, w_in, layer, col_tile0, gains, cos_t, sin_t, heads, emit_f32):
    m, k = xn.shape
    h_moba, h_sb, h_fox = heads
    w_mix = (h_moba + h_sb + h_fox) * HEAD_DIM
    per = PROJ_TN // HEAD_DIM
    assert h_moba % per == 0 and h_sb % per == 0 and h_fox % per == 0
    n_rope, n_plain_end, n_tiles = h_moba // per, (h_moba + h_sb) // per, w_mix // PROJ_TN
    assert m % cos_t.shape[0] == 0
    tm = _tile(cos_t.shape[0], 1024)
    n_tbl = cos_t.shape[0] // tm
    out_shape = [jax.ShapeDtypeStruct((m, w_mix), BF16)]
    out_specs = [pl.BlockSpec((tm, PROJ_TN), lambda i, j: (i, j))]
    if emit_f32:
        out_shape.append(jax.ShapeDtypeStruct((m, w_mix), F32))
        out_specs.append(pl.BlockSpec((tm, PROJ_TN), lambda i, j: (i, j)))
    return pl.pallas_call(
        functools.partial(_proj_qk_body, n_rope=n_rope, n_plain_end=n_plain_end),
        grid=(m // tm, n_tiles),
        in_specs=[pl.BlockSpec((tm, k), lambda i, j: (i, 0)),
                  _w_in_spec(w_in, layer, lambda i, j: col_tile0 + j),
                  pl.BlockSpec((None, 1, HEAD_DIM), lambda i, j: (jnp.where(j >= n_plain_end, 1, 0), 0, 0)),
                  pl.BlockSpec((tm, HEAD_DIM), lambda i, j: (i % n_tbl, 0)),
                  pl.BlockSpec((tm, HEAD_DIM), lambda i, j: (i % n_tbl, 0))],
        out_specs=out_specs,
        out_shape=out_shape,
        compiler_params=_cparams("parallel", "arbitrary"),
        name="proj_qk",
    )(xn, w_in, gains, cos_t, sin_t)


def _proj_v_body(x_ref, w_ref, o_bf_ref, o_f32_ref):
    acc = _dot(x_ref[...], w_ref[...])
    o_bf_ref[...] = acc.astype(BF16)
    o_f32_ref[...] = acc


def _proj_v(xn, w_in, layer, col_tile0, n_cols):
    m, k = xn.shape
    tm = _tile(m, 1024)
    spec = pl.BlockSpec((tm, PROJ_TN), lambda i, j: (i, j))
    return pl.pallas_call(
        _proj_v_body,
        grid=(m // tm, n_cols // PROJ_TN),
        in_specs=[pl.BlockSpec((tm, k), lambda i, j: (i, 0)),
                  _w_in_spec(w_in, layer, lambda i, j: col_tile0 + j)],
        out_specs=[spec, spec],
        out_shape=[jax.ShapeDtypeStruct((m, n_cols), BF16), jax.ShapeDtypeStruct((m, n_cols), F32)],
        compiler_params=_cparams("parallel", "arbitrary"),
        name="proj_v",
    )(xn, w_in)


def _proj_gate_body(x_ref, w_ref, o_ref):
    o_ref[...] = jax.nn.sigmoid(_dot(x_ref[...], w_ref[...]))


def _proj_gate(xn, w_in, layer, col_tile0, n_cols):
    m, k = xn.shape
    tm = _tile(m, 1024)
    return pl.pallas_call(
        _proj_gate_body,
        grid=(m // tm, n_cols // PROJ_TN),
        in_specs=[pl.BlockSpec((tm, k), lambda i, j: (i, 0)),
                  _w_in_spec(w_in, layer, lambda i, j: col_tile0 + j)],
        out_specs=pl.BlockSpec((tm, PROJ_TN), lambda i, j: (i, j)),
        out_shape=jax.ShapeDtypeStruct((m, n_cols), F32),
        compiler_params=_cparams("parallel", "arbitrary"),
        name="proj_gate",
    )(xn, w_in)


def _proj_logf_body(x_ref, w_ref, b_ref, o_ref, *, n_valid):
    acc = _dot(x_ref[...], w_ref[...])[:, :LANES]
    lane = lax.broadcasted_iota(jnp.int32, acc.shape, 1)
    ls, _ = _log_sigmoid_pair(acc + b_ref[...])
    o_ref[...] = jnp.where(lane < n_valid, ls, 0.0)


def _proj_logf(xn, w_in, layer, col_tile0, b_pad, n_valid):
    m, k = xn.shape
    tm = _tile(m, 1024)
    return pl.pallas_call(
        functools.partial(_proj_logf_body, n_valid=n_valid),
        grid=(m // tm,),
        in_specs=[pl.BlockSpec((tm, k), lambda i: (i, 0)),
                  _w_in_spec(w_in, layer, lambda i: col_tile0),
                  pl.BlockSpec((1, LANES), lambda i: (0, 0))],
        out_specs=pl.BlockSpec((tm, LANES), lambda i: (i, 0)),
        out_shape=jax.ShapeDtypeStruct((m, LANES), F32),
        compiler_params=_cparams("parallel"),
        name="proj_logf",
    )(xn, w_in, b_pad)


def _merge_body(oa_ref, ob_ref, oc_ref, wa_ref, wb_ref, wc_ref, ga_ref, gb_ref, gc_ref, o_ref):
    merged = (ga_ref[...] * _dot(oa_ref[...].astype(BF16), wa_ref[...])
              + gb_ref[...] * _dot(ob_ref[...].astype(BF16), wb_ref[...])
              + gc_ref[...] * _dot(oc_ref[...].astype(BF16), wc_ref[...]))
    o_ref[...] = merged.astype(o_ref.dtype)


def _merge(o_a, o_b, o_c, wb_a, wb_b, wb_c, gates):
    m = o_a.shape[0]
    d = wb_a.shape[1]
    tm, tn = _tile(m, 1024), _tile(d, 512)
    nd = d // tn

    def o_spec(o):
        return pl.BlockSpec((tm, o.shape[1]), lambda i, j: (i, 0))

    def w_spec(w):
        return pl.BlockSpec((w.shape[0], tn), lambda i, j: (0, j))

    def g_spec(b):
        return pl.BlockSpec((tm, tn), lambda i, j: (i, b * nd + j))

    return pl.pallas_call(
        _merge_body,
        grid=(m // tm, nd),
        in_specs=[o_spec(o_a), o_spec(o_b), o_spec(o_c), w_spec(wb_a), w_spec(wb_b), w_spec(wb_c),
                  g_spec(0), g_spec(1), g_spec(2)],
        out_specs=pl.BlockSpec((tm, tn), lambda i, j: (i, j)),
        out_shape=jax.ShapeDtypeStruct((m, d), BF16),
        compiler_params=_cparams("parallel", "arbitrary"),
        name="merge",
    )(o_a, o_b, o_c, wb_a, wb_b, wb_c, gates, gates, gates)


def _rope_table_body(pos_ref, inv_ref, cos_ref, sin_ref):
    ang = pos_ref[...] * inv_ref[...]
    lane = lax.broadcasted_iota(jnp.int32, ang.shape, 1)
    cos_ref[...] = jnp.cos(ang)
    sin_ref[...] = jnp.where(lane < HEAD_DIM // 2, -1.0, 1.0) * jnp.sin(ang)


def _rope_tables(pos):
    half = HEAD_DIM // 2
    inv = ROPE_THETA ** (-jnp.arange(half, dtype=F32) / half)
    inv_full = jnp.concatenate([inv, inv]).reshape(1, HEAD_DIM)
    r = pos.shape[0]
    spec = pl.BlockSpec((r, HEAD_DIM), lambda: (0, 0))
    return pl.pallas_call(
        _rope_table_body,
        in_specs=[pl.BlockSpec((r, 1), lambda: (0, 0)), pl.BlockSpec((1, HEAD_DIM), lambda: (0, 0))],
        out_specs=[spec, spec],
        out_shape=[jax.ShapeDtypeStruct((r, HEAD_DIM), F32)] * 2,
        name="rope_tables",
    )(pos.astype(F32).reshape(r, 1), inv_full)


def _tri(n, kind):
    r = lax.broadcasted_iota(jnp.int32, (n, n), 0)
    c = lax.broadcasted_iota(jnp.int32, (n, n), 1)
    return {"row_ge_col": r >= c, "row_gt_col": r > c}[kind].astype(BF16)


def _cumsum_prompt_body(x_ref, tri_ref, fcol_ref, frow_ref, *, n_chunks):
    tri = tri_ref[...]
    c_len = tri.shape[0]
    carry = jnp.zeros((1, LANES), F32)
    for c in range(n_chunks):
        f = _tri_dot_left(tri, x_ref[c * c_len:(c + 1) * c_len, :]) + carry
        fcol_ref[c * c_len:(c + 1) * c_len, :] = f
        frow_ref[:, c * c_len:(c + 1) * c_len] = f.T
        carry = f[c_len - 1:c_len, :]


def _cumsum_prompt(logf, batch, t):
    c_len = MOBA_BLOCK
    return pl.pallas_call(
        functools.partial(_cumsum_prompt_body, n_chunks=t // c_len),
        grid=(batch,),
        in_specs=[pl.BlockSpec((t, LANES), lambda b: (b, 0)),
                  pl.BlockSpec((c_len, c_len), lambda b: (0, 0))],
        out_specs=[pl.BlockSpec((t, LANES), lambda b: (b, 0)),
                   pl.BlockSpec((None, LANES, t), lambda b: (b, 0, 0))],
        out_shape=[jax.ShapeDtypeStruct((batch * t, LANES), F32),
                   jax.ShapeDtypeStruct((batch, LANES, t), F32)],
        compiler_params=_cparams("parallel"),
        name="cumsum_prompt",
    )(logf, _tri(c_len, "row_ge_col"))


CUMSUM_PAGES_PER_STEP = 8


def _cumsum_pages_body(pt_ref, *refs, n_heads, n_in):
    x_refs = refs[:n_in]
    tri_ref, f_ref, pad_ref, carry_ref = refs[n_in:]
    page = tri_ref.shape[0]

    @pl.when(pl.program_id(1) == 0)
    def _():
        carry_ref[...] = jnp.zeros_like(carry_ref)

    pad_ref[...] = jnp.zeros_like(pad_ref)
    carry = carry_ref[...]
    for e in range(n_in):
        pad_ref[:, :n_heads] = x_refs[e][...]
        f = _tri_dot_left(tri_ref[...], pad_ref[...]) + carry
        f_ref[e * page:(e + 1) * page, :] = f
        carry = f[page - 1:, :]
    carry_ref[...] = carry


def _cumsum_pages(cache_logf, layer, page_table):
    _, _, page, n_heads = cache_logf.shape
    db, n_pages = page_table.shape
    assert page == LANES
    n_in = CUMSUM_PAGES_PER_STEP
    while n_pages % n_in:
        n_in //= 2

    def page_spec(e):
        return pl.BlockSpec((None, None, page, n_heads), lambda b, p, pt: (layer, pt[b, p * n_in + e], 0, 0))

    grid_spec = pltpu.PrefetchScalarGridSpec(
        num_scalar_prefetch=1,
        grid=(db, n_pages // n_in),
        in_specs=[page_spec(e) for e in range(n_in)] + [pl.BlockSpec((page, page), lambda b, p, pt: (0, 0))],
        out_specs=pl.BlockSpec((None, n_in * page, LANES), lambda b, p, pt: (b, p, 0)),
        scratch_shapes=[pltpu.VMEM((page, LANES), F32), pltpu.VMEM((1, LANES), F32)],
    )
    return pl.pallas_call(
        functools.partial(_cumsum_pages_body, n_heads=n_heads, n_in=n_in),
        grid_spec=grid_spec,
        out_shape=jax.ShapeDtypeStruct((db, n_pages * page, LANES), F32),
        compiler_params=_cparams("parallel", "arbitrary"),
        name="cumsum_pages",
    )(page_table, *([cache_logf] * n_in), _tri(page, "row_ge_col"))


HEADS_PER_STEP = 2


def _head_lanes(e):
    return slice(e * HEAD_DIM, (e + 1) * HEAD_DIM)


def _causal_iotas(n):
    row = lax.broadcasted_iota(jnp.int32, (n, n), 0)
    col = lax.broadcasted_iota(jnp.int32, (n, n), 1)
    return row, col


def _moba_select(q, kmean, i, nb):
    shift = max(nb - 1, 1).bit_length()
    width = 1 << shift
    assert width * width <= LANES

    def hi_part(x):
        return lax.shift_right_logical(x, shift)

    def lo_part(x):
        return jnp.bitwise_and(x, width - 1)

    gs = lax.dot_general(q.astype(F32), kmean, _NT, precision=_HI, preferred_element_type=F32)
    r = lax.broadcasted_iota(jnp.int32, (LANES, LANES), 0)
    c = lax.broadcasted_iota(jnp.int32, (LANES, LANES), 1)
    pair_ok = c < width * width
    diff_mat = jnp.where(pair_ok, jnp.where(r == hi_part(c), 1.0, 0.0) - jnp.where(r == lo_part(c), 1.0, 0.0), 0.0)
    diff = jnp.dot(gs, diff_mat, precision=_HI, preferred_element_type=F32)
    lane = lax.broadcasted_iota(jnp.int32, diff.shape, 1)
    n_l, m_l = hi_part(lane), lo_part(lane)
    beats = jnp.logical_or(diff < 0.0, jnp.logical_and(diff == 0.0, m_l < n_l))
    beats = jnp.logical_and(beats, jnp.logical_and(m_l < i, lane < width * width))
    sum_mat = jnp.where(jnp.logical_and(hi_part(r) == c, r < width * width), 1.0, 0.0).astype(BF16)
    rank = _dot(jnp.where(beats, 1.0, 0.0).astype(BF16), sum_mat)
    return jnp.where(jnp.logical_and(rank < MOBA_TOPK, lane < i), 1.0, 0.0)


def _moba_prompt_body(q_ref, k_ref, v_ref, o_ref, kmean_ref, *, nb, scale):
    i = pl.program_id(2)
    blk = MOBA_BLOCK

    @pl.when(i == 0)
    def _():
        kmean_ref[...] = jnp.zeros_like(kmean_ref)
        for e in range(HEADS_PER_STEP):
            for n in range(nb):
                kmean_ref[e, n:n + 1, :] = jnp.mean(k_ref[n * blk:(n + 1) * blk, _head_lanes(e)].astype(F32),
                                                    axis=0, keepdims=True)

    start = pl.multiple_of(i * blk, blk)
    row, col = _causal_iotas(blk)
    causal = col <= row
    qs = [q_ref[:, _head_lanes(e)] for e in range(HEADS_PER_STEP)]
    sels = [_moba_select(qs[e], kmean_ref[e], i, nb) for e in range(HEADS_PER_STEP)]
    lane = lax.broadcasted_iota(jnp.int32, sels[0].shape, 1)

    def own(e):
        s = jnp.where(causal, _dot_nt(qs[e], k_ref[pl.ds(start, blk), _head_lanes(e)]) * scale, -jnp.inf)
        m0 = jnp.max(s, axis=-1, keepdims=True)
        p = jnp.exp(s - m0)
        return m0, jnp.sum(p, axis=-1, keepdims=True), _dot(p.astype(BF16), v_ref[pl.ds(start, blk), _head_lanes(e)])

    def past(n, carry):
        st = pl.multiple_of(n * blk, blk)
        raws = [_dot_nt(qs[e], k_ref[pl.ds(st, blk), _head_lanes(e)]) for e in range(HEADS_PER_STEP)]
        out = []
        for e in range(HEADS_PER_STEP):
            m, l, acc = carry[e]
            chosen = jnp.sum(jnp.where(lane == n, sels[e], 0.0), axis=-1, keepdims=True) > 0.5
            s = jnp.where(chosen, raws[e] * scale, -jnp.inf)
            m_new = jnp.maximum(m, jnp.max(s, axis=-1, keepdims=True))
            alpha = jnp.exp(m - m_new)
            p = jnp.exp(s - m_new)
            l = alpha * l + jnp.sum(p, axis=-1, keepdims=True)
            acc = alpha * acc + _dot(p.astype(BF16), v_ref[pl.ds(st, blk), _head_lanes(e)])
            out.append((m_new, l, acc))
        return tuple(out)

    fin = lax.fori_loop(0, i, past, tuple(own(e) for e in range(HEADS_PER_STEP)))
    o_ref[...] = jnp.concatenate([acc / l for _, l, acc in fin], axis=-1).astype(o_ref.dtype)


def _sb_prompt_body(q_ref, k_ref, v_ref, tri_ref, o_ref, *, scale):
    i = pl.program_id(2)
    blk = Q_TILE
    qs = [q_ref[:, _head_lanes(e)] for e in range(HEADS_PER_STEP)]
    tri = tri_ref[...]

    def logits(e, st):
        return _dot_nt(qs[e], k_ref[pl.ds(st, blk), _head_lanes(e)]) * scale

    def block(e, st, z, carry, acc, strict):
        ls, lsn = _log_sigmoid_pair(z)
        if strict is not None:
            lsn = jnp.where(strict, lsn, 0.0)
        hi = lsn.astype(BF16)
        lo = (lsn - hi.astype(F32)).astype(BF16)
        later = _dot(hi, tri) + _dot(lo, tri)
        w = jnp.exp(ls + later + carry)
        if strict is not None:
            w = jnp.where(strict, w, 0.0)
        acc = acc + _dot(w.astype(BF16), v_ref[pl.ds(st, blk), _head_lanes(e)])
        return carry + jnp.sum(lsn, axis=-1, keepdims=True), acc

    row, col = _causal_iotas(blk)
    start = pl.multiple_of(i * blk, blk)
    zs = [logits(e, start) for e in range(HEADS_PER_STEP)]
    init = tuple(block(e, start, zs[e], jnp.zeros((blk, 1), F32),
                       jnp.zeros((blk, HEAD_DIM), F32), col < row) for e in range(HEADS_PER_STEP))

    def past(t, c):
        st = pl.multiple_of((i - 1 - t) * blk, blk)
        zs = [logits(e, st) for e in range(HEADS_PER_STEP)]
        return tuple(block(e, st, zs[e], c[e][0], c[e][1], None) for e in range(HEADS_PER_STEP))

    fin = lax.fori_loop(0, i, past, init)
    o_ref[...] = jnp.concatenate([acc for _, acc in fin], axis=-1).astype(o_ref.dtype)


def _fox_prompt_body(q_ref, k_ref, v_ref, fcol_ref, frow_ref, o_ref, *, scale):
    h = pl.program_id(1)
    i = pl.program_id(2)
    blk = Q_TILE
    qs = [q_ref[:, _head_lanes(e)] for e in range(HEADS_PER_STEP)]
    fcol = fcol_ref[...]
    lane = lax.broadcasted_iota(jnp.int32, fcol.shape, 1)
    fqs = [jnp.sum(jnp.where(lane == h * HEADS_PER_STEP + e, fcol, 0.0), axis=-1, keepdims=True)
           for e in range(HEADS_PER_STEP)]

    def scores(e, st):
        fk = frow_ref[pl.ds(h * HEADS_PER_STEP + e, 1), pl.ds(st, blk)]
        return _dot_nt(qs[e], k_ref[pl.ds(st, blk), _head_lanes(e)]) * scale + (fqs[e] - fk)

    start = pl.multiple_of(i * blk, blk)
    row, col = _causal_iotas(blk)
    causal = col <= row

    def own(e):
        s = jnp.where(causal, scores(e, start), -jnp.inf)
        m0 = jnp.max(s, axis=-1, keepdims=True)
        p = jnp.exp(s - m0)
        return m0, jnp.sum(p, axis=-1, keepdims=True), _dot(p.astype(BF16), v_ref[pl.ds(start, blk), _head_lanes(e)])

    def past(n, carry):
        st = pl.multiple_of(n * blk, blk)
        ss = [scores(e, st) for e in range(HEADS_PER_STEP)]
        out = []
        for e in range(HEADS_PER_STEP):
            m, l, acc = carry[e]
            s = ss[e]
            m_new = jnp.maximum(m, jnp.max(s, axis=-1, keepdims=True))
            alpha = jnp.exp(m - m_new)
            p = jnp.exp(s - m_new)
            l = alpha * l + jnp.sum(p, axis=-1, keepdims=True)
            acc = alpha * acc + _dot(p.astype(BF16), v_ref[pl.ds(st, blk), _head_lanes(e)])
            out.append((m_new, l, acc))
        return tuple(out)

    fin = lax.fori_loop(0, i, past, tuple(own(e) for e in range(HEADS_PER_STEP)))
    o_ref[...] = jnp.concatenate([acc / l for _, l, acc in fin], axis=-1).astype(o_ref.dtype)


def _prompt_attention(q_bf, k_bf, v_bf, fcol, frow, batch, t, heads):
    h_moba, h_sb, h_fox = heads
    assert t % Q_TILE == 0 and Q_TILE == MOBA_BLOCK
    nq = t // Q_TILE
    m = batch * t
    scale = HEAD_DIM ** -0.5
    hps = HEADS_PER_STEP
    assert h_moba % hps == 0 and h_sb % hps == 0 and h_fox % hps == 0
    width = hps * HEAD_DIM

    def specs(h0):
        qs = pl.BlockSpec((Q_TILE, width), lambda b, h, i: (b * nq + i, h0 // hps + h))
        ks = pl.BlockSpec((t, width), lambda b, h, i: (b, h0 // hps + h))
        return qs, ks, ks

    def out(n_heads):
        return (pl.BlockSpec((Q_TILE, width), lambda b, h, i: (b * nq + i, h)),
                jax.ShapeDtypeStruct((m, n_heads * HEAD_DIM), BF16))

    sem = _cparams("parallel", "parallel", "arbitrary")

    o_spec, o_shape = out(h_moba)
    o_a = pl.pallas_call(
        functools.partial(_moba_prompt_body, nb=nq, scale=scale),
        grid=(batch, h_moba // hps, nq),
        in_specs=list(specs(0)),
        out_specs=o_spec, out_shape=o_shape,
        scratch_shapes=[pltpu.VMEM((hps, LANES, HEAD_DIM), F32)],
        compiler_params=sem, name="moba_prompt",
    )(q_bf, k_bf, v_bf)

    o_spec, o_shape = out(h_sb)
    o_b = pl.pallas_call(
        functools.partial(_sb_prompt_body, scale=scale),
        grid=(batch, h_sb // hps, nq),
        in_specs=list(specs(h_moba)) + [pl.BlockSpec((Q_TILE, Q_TILE), lambda b, h, i: (0, 0))],
        out_specs=o_spec, out_shape=o_shape,
        compiler_params=sem, name="sb_prompt",
    )(q_bf, k_bf, v_bf, _tri(Q_TILE, "row_gt_col"))

    o_spec, o_shape = out(h_fox)
    o_c = pl.pallas_call(
        functools.partial(_fox_prompt_body, scale=scale),
        grid=(batch, h_fox // hps, nq),
        in_specs=list(specs(h_moba + h_sb)) + [
            pl.BlockSpec((Q_TILE, LANES), lambda b, h, i: (b * nq + i, 0)),
            pl.BlockSpec((None, LANES, t), lambda b, h, i: (b, 0, 0))],
        out_specs=o_spec, out_shape=o_shape,
        compiler_params=sem, name="fox_prompt",
    )(q_bf, k_bf, v_bf, fcol, frow)
    return o_a, o_b, o_c


def _sample_attn_body(pt_ref, q_ref, kn_ref, vn_ref, lfn_ref, fexp_ref, ftot_ref, tri_ref, utile_ref, *refs,
                      heads, n_groups, n_pages, scale, n_new):
    h_moba, h_sb, h_fox = heads
    h_tot = h_moba + h_sb + h_fox
    (k_ref, v_ref, o_ref, kpad_ref, vpad_ref, acc_ref, m_ref, l_ref, carry_ref, fq_ref, cumn_t_ref,
     pm_ref, pl_ref, po_ref, pg_ref, gsb_ref) = refs

    r = pl.program_id(1)
    page = LANES
    rows = q_ref.shape[0]
    pages_per_block = MOBA_BLOCK // page
    grp_rows = HEAD_GROUP * rows
    grp_keys = HEAD_GROUP * page

    def q_head(h):
        return q_ref[:, h * HEAD_DIM:(h + 1) * HEAD_DIM].astype(BF16)

    def bcast(x):
        return jnp.broadcast_to(x, (x.shape[0], LANES))

    def head_rows(h):
        return slice(h * rows, (h + 1) * rows)

    @pl.when(r == 0)
    def _():
        kpad_ref[...] = jnp.zeros_like(kpad_ref)
        vpad_ref[...] = jnp.zeros_like(vpad_ref)
        kpad_ref[0:n_new, :] = kn_ref[...]
        vpad_ref[0:n_new, :] = vn_ref[...]
        row = lax.broadcasted_iota(jnp.int32, (rows, page), 0)
        col = lax.broadcasted_iota(jnp.int32, (rows, page), 1)
        weak = col <= row
        strict = col < row

        cumn_t_ref[...] = jnp.zeros_like(cumn_t_ref)
        cumn_t_ref[0:rows, :] = lfn_ref[...]
        lfn_pad = cumn_t_ref[...]
        cum_pad = _tri_dot_left(tri_ref[...], lfn_pad) + lfn_pad
        cumn = cum_pad[0:rows, :]
        cumn_t_ref[...] = cum_pad.T

        for h in range(h_tot):
            hr = head_rows(h)
            kh = kpad_ref[:, h * HEAD_DIM:(h + 1) * HEAD_DIM].astype(BF16)
            vh = vpad_ref[:, h * HEAD_DIM:(h + 1) * HEAD_DIM].astype(BF16)
            s_raw = _dot_nt(q_head(h), kh)
            if h < h_moba:
                s = jnp.where(weak, s_raw * scale, -jnp.inf)
                m = jnp.max(s, axis=-1, keepdims=True)
                p = jnp.exp(s - m)
                pm_ref[n_pages, hr] = bcast(m)
                pl_ref[n_pages, hr] = bcast(jnp.sum(p, axis=-1, keepdims=True))
                po_ref[n_pages, hr] = _dot(p.astype(BF16), vh)
            elif h < h_moba + h_sb:
                ls, lsn = _log_sigmoid_pair(s_raw * scale)
                lsn = jnp.where(strict, lsn, 0.0)
                later = _tri_dot_right(lsn, tri_ref[...])
                w = jnp.where(strict, jnp.exp(ls + later), 0.0)
                acc_ref[hr] = _dot(w.astype(BF16), vh)
                carry_ref[hr] = bcast(jnp.sum(lsn, axis=-1, keepdims=True))
            else:
                c = h - h_moba - h_sb
                bias = cumn[:, c:c + 1] - cumn_t_ref[c:c + 1, :]
                s = jnp.where(weak, s_raw * scale + bias, -jnp.inf)
                m = jnp.max(s, axis=-1, keepdims=True)
                p = jnp.exp(s - m)
                m_ref[hr] = bcast(m)
                l_ref[hr] = bcast(jnp.sum(p, axis=-1, keepdims=True))
                acc_ref[hr] = _dot(p.astype(BF16), vh)
                fq_ref[hr] = bcast(cumn[:, c:c + 1] + ftot_ref[0:1, c:c + 1])

    lane = lax.broadcasted_iota(jnp.int32, (grp_rows, grp_keys), 1)
    qrow = lax.broadcasted_iota(jnp.int32, (grp_rows, grp_keys), 0)
    own = jnp.bitwise_and(lane, HEAD_GROUP - 1) == lax.shift_right_logical(qrow, 3)
    assert rows == 8 and HEAD_GROUP == 8
    n_tiles = grp_keys // LANES

    raws = []
    for g in range(n_groups):
        h0 = g * HEAD_GROUP
        qg = jnp.concatenate([q_ref[:, (h0 + hh) * HEAD_DIM:(h0 + hh + 1) * HEAD_DIM]
                              for hh in range(HEAD_GROUP)], axis=0).astype(BF16)
        kb = k_ref[:, h0:h0 + HEAD_GROUP, :].reshape(grp_keys, HEAD_DIM).astype(BF16)
        raws.append(_dot_nt(qg, kb))

    for g in range(n_groups):
        h0 = g * HEAD_GROUP
        n_m = min(max(h_moba - h0, 0), HEAD_GROUP)
        n_s = min(max(h_moba + h_sb - h0, 0), HEAD_GROUP) - n_m
        r_m, r_s = n_m * rows, (n_m + n_s) * rows
        base = h0 * rows
        vb = v_ref[:, h0:h0 + HEAD_GROUP, :].reshape(grp_keys, HEAD_DIM).astype(BF16)
        raw = raws[g]
        probs = []

        if r_m > 0:
            sl = slice(base, base + r_m)
            raw_m, own_m = raw[0:r_m], own[0:r_m]
            s = jnp.where(own_m, raw_m * scale, -jnp.inf)
            m = jnp.max(s, axis=-1, keepdims=True)
            p = jnp.exp(s - m)
            pm_ref[r, sl] = bcast(m)
            pl_ref[r, sl] = bcast(jnp.sum(p, axis=-1, keepdims=True))
            pg_ref[r, sl] = bcast(jnp.sum(jnp.where(own_m, raw_m, 0.0), axis=-1, keepdims=True))
            probs.append(p)

        if r_s > r_m:
            sl = slice(base + r_m, base + r_s)
            n_r = r_s - r_m
            own_s = own[r_m:r_s]
            ls, lsn = _log_sigmoid_pair(raw[r_m:r_s] * scale)
            lsn = jnp.where(own_s, lsn, 0.0)
            stacked = jnp.concatenate([lsn[:, t * LANES:(t + 1) * LANES] for t in range(n_tiles)], axis=0)
            hi = stacked.astype(BF16)
            lo = (stacked - hi.astype(F32)).astype(BF16)
            intra = _dot(hi, utile_ref[...]) + _dot(lo, utile_ref[...])
            carry = carry_ref[sl][:, 0:1]
            suffix = carry
            later = [None] * n_tiles
            for t in reversed(range(n_tiles)):
                later[t] = intra[t * n_r:(t + 1) * n_r] + suffix
                suffix = suffix + jnp.sum(lsn[:, t * LANES:(t + 1) * LANES], axis=-1, keepdims=True)
            carry_ref[sl] = bcast(suffix)
            w = jnp.where(own_s, jnp.exp(ls + jnp.concatenate(later, axis=-1)), 0.0)
            probs.append(w)

        if r_s < grp_rows:
            sl = slice(base + r_s, base + grp_rows)
            own_f = own[r_s:]
            bias = fq_ref[sl][:, 0:1] - fexp_ref[g:g + 1, :]
            s = jnp.where(own_f, raw[r_s:] * scale + bias, -jnp.inf)
            m_old = m_ref[sl][:, 0:1]
            m_new = jnp.maximum(m_old, jnp.max(s, axis=-1, keepdims=True))
            alpha = jnp.exp(m_old - m_new)
            p = jnp.exp(s - m_new)
            l_ref[sl] = alpha * l_ref[sl] + bcast(jnp.sum(p, axis=-1, keepdims=True))
            m_ref[sl] = bcast(m_new)
            probs.append(p)

        out = _dot(jnp.concatenate(probs, axis=0).astype(BF16), vb)
        if r_m > 0:
            po_ref[r, base:base + r_m] = out[0:r_m]
        if r_s > r_m:
            sl = slice(base + r_m, base + r_s)
            acc_ref[sl] = acc_ref[sl] + out[r_m:r_s]
        if r_s < grp_rows:
            sl = slice(base + r_s, base + grp_rows)
            acc_ref[sl] = alpha * acc_ref[sl] + out[r_s:]

    @pl.when(r == n_pages - 1)
    def _():
        n_blocks = n_pages // pages_per_block
        moba_rows = h_moba * rows

        def fill(n, _):
            tot = pg_ref[n_pages - 1 - n * pages_per_block]
            for e in range(1, pages_per_block):
                tot = tot + pg_ref[n_pages - 1 - (n * pages_per_block + e)]
            gsb_ref[n] = tot * (1.0 / MOBA_BLOCK)
            return 0

        lax.fori_loop(0, n_blocks, fill, 0)

        def add_block(n, carry):
            m_run, l_run, o_run = carry
            gs_n = gsb_ref[n]

            def count(mb, rank):
                gs_m = gsb_ref[mb]
                beats = jnp.logical_or(gs_m > gs_n, jnp.logical_and(gs_m == gs_n, mb < n))
                return rank + jnp.where(beats, 1.0, 0.0)

            rank = lax.fori_loop(0, n_blocks, count, jnp.zeros((moba_rows, LANES), F32))
            chosen = rank < MOBA_TOPK
            for e in range(pages_per_block):
                slot = n_pages - 1 - (n * pages_per_block + e)
                m_p = jnp.where(chosen, pm_ref[slot], -jnp.inf)
                m_new = jnp.maximum(m_run, m_p)
                a_run = jnp.exp(m_run - m_new)
                a_p = jnp.exp(m_p - m_new)
                l_run = a_run * l_run + a_p * pl_ref[slot]
                o_run = a_run * o_run + a_p * po_ref[slot]
                m_run = m_new
            return m_run, l_run, o_run

        _, l_run, o_run = lax.fori_loop(0, n_blocks, add_block,
                                        (pm_ref[n_pages], pl_ref[n_pages], po_ref[n_pages]))
        moba_out = o_run / l_run
        for h in range(h_tot):
            hr = head_rows(h)
            if h < h_moba:
                res = moba_out[hr]
            elif h < h_moba + h_sb:
                res = acc_ref[hr]
            else:
                res = acc_ref[hr] / l_ref[hr]
            o_ref[:, h * HEAD_DIM:(h + 1) * HEAD_DIM] = res


def _sample_attention(q_s, k_new, v_new, logf_new, fcol_past, cache_k, cache_v, layer, page_table, heads):
    h_moba, h_sb, h_fox = heads
    h_tot = h_moba + h_sb + h_fox
    db, n_pages = page_table.shape
    n_new = q_s.shape[0] // db
    _, n_pool, page, hh, dd = cache_k.shape
    assert (page, hh, dd) == (LANES, h_tot, HEAD_DIM) and h_tot % HEAD_GROUP == 0
    assert n_new == SUBLANES and (n_pages * page) % MOBA_BLOCK == 0 and MOBA_BLOCK % page == 0
    n_groups = h_tot // HEAD_GROUP
    w = h_tot * HEAD_DIM
    n_rows = h_tot * n_new
    grp_keys = HEAD_GROUP * page

    zero_lane = LANES - 1
    assert h_fox < LANES
    lane_of_head = [h - h_moba - h_sb if h >= h_moba + h_sb else zero_lane for h in range(h_tot)]
    fexp = fcol_past[:, :, jnp.array(lane_of_head, jnp.int32)]
    fexp = fexp.reshape(db, n_pages, page, n_groups, HEAD_GROUP).transpose(0, 1, 3, 2, 4)
    fexp = fexp.reshape(db, n_pages, n_groups, grp_keys)
    ftot = fcol_past[:, n_pages * page - 1:, :]

    lane = jnp.arange(LANES, dtype=jnp.int32)
    utile = (lane[:, None] // HEAD_GROUP > lane[None, :] // HEAD_GROUP).astype(BF16)

    def row_spec(width):
        return pl.BlockSpec((n_new, width), lambda b, r, pt: (b, 0))

    def const_spec():
        return pl.BlockSpec((LANES, LANES), lambda b, r, pt: (0, 0))

    cache_spec = pl.BlockSpec((None, None, page, h_tot, HEAD_DIM),
                              lambda b, r, pt: (layer, pt[b, n_pages - 1 - r], 0, 0, 0))

    grid_spec = pltpu.PrefetchScalarGridSpec(
        num_scalar_prefetch=1,
        grid=(db, n_pages),
        in_specs=[row_spec(w), row_spec(w), row_spec(w), row_spec(LANES),
                  pl.BlockSpec((None, None, n_groups, grp_keys), lambda b, r, pt: (b, n_pages - 1 - r, 0, 0)),
                  pl.BlockSpec((None, 1, LANES), lambda b, r, pt: (b, 0, 0)),
                  const_spec(), const_spec()]
                 + [cache_spec, cache_spec],
        out_specs=row_spec(w),
        scratch_shapes=[
            pltpu.VMEM((page, w), F32), pltpu.VMEM((page, w), F32),
            pltpu.VMEM((n_rows, HEAD_DIM), F32),
            pltpu.VMEM((n_rows, LANES), F32),
            pltpu.VMEM((n_rows, LANES), F32),
            pltpu.VMEM((n_rows, LANES), F32),
            pltpu.VMEM((n_rows, LANES), F32),
            pltpu.VMEM((LANES, LANES), F32),
            pltpu.VMEM((n_pages + 1, h_moba * n_new, LANES), F32),
            pltpu.VMEM((n_pages + 1, h_moba * n_new, LANES), F32),
            pltpu.VMEM((n_pages + 1, h_moba * n_new, HEAD_DIM), F32),
            pltpu.VMEM((n_pages, h_moba * n_new, LANES), F32),
            pltpu.VMEM((n_pages // (MOBA_BLOCK // page), h_moba * n_new, LANES), F32),
        ],
    )
    body = functools.partial(_sample_attn_body, heads=heads, n_groups=n_groups, n_pages=n_pages,
                             scale=HEAD_DIM ** -0.5, n_new=n_new)
    return pl.pallas_call(
        body,
        grid_spec=grid_spec,
        out_shape=jax.ShapeDtypeStruct((db * n_new, w), F32),
        compiler_params=_cparams("parallel", "arbitrary"),
        name="sample_attn",
    )(page_table, q_s, k_new, v_new, logf_new, fexp, ftot, _tri(page, "row_gt_col"), utile,
      cache_k, cache_v)


def _layer_common(x, w, heads, cos_t, sin_t):
    h = _mm_resid(_ffn_up(_rmsnorm(x, w["g_f1"]), w["f1_gate"], w["f1_up"]), w["f1_down"], x, 0.5)
    n = _rmsnorm(h, w["g_mix"])
    w_mix = sum(heads) * HEAD_DIM
    d = x.shape[1]
    t0 = w_mix // PROJ_TN
    w_in, layer = w["w_in"], w["layer"]
    q_bf, = _proj_qk(n, w_in, layer, 0, w["q_gains"], cos_t, sin_t, heads, False)
    k_bf, k_f32 = _proj_qk(n, w_in, layer, t0, w["k_gains"], cos_t, sin_t, heads, True)
    v_bf, v_f32 = _proj_v(n, w_in, layer, 2 * t0, w_mix)
    gates = _proj_gate(n, w_in, layer, 3 * t0, 3 * d)
    logf = _proj_logf(n, w_in, layer, 3 * t0 + 3 * d // PROJ_TN, w["b_f"], heads[2])
    return h, q_bf, k_bf, k_f32, v_bf, v_f32, gates, logf


def _layer_tail(h, o_a, o_b, o_c, gates, w):
    merged = _merge(o_a, o_b, o_c, w["wb_a"], w["wb_b"], w["wb_c"], gates)
    h = _mm_resid(merged, w["w_out"], h, 1.0)
    return _mm_resid(_ffn_up(_rmsnorm(h, w["g_f2"]), w["f2_gate"], w["f2_up"]), w["f2_down"], h, 0.5)


def kernel(x_prompt, x_sample, cache_k, cache_v, cache_logf, page_table, norm_ffn1, ffn1_w_gate, ffn1_w_up, ffn1_w_down, norm_mix, w_in, b_forget, q_norm_moba, k_norm_moba, q_norm_fox, k_norm_fox, w_branch_moba, w_branch_sb, w_branch_fox, w_out, norm_ffn2, ffn2_w_gate, ffn2_w_up, ffn2_w_down):
    batch, t, d = x_prompt.shape
    db, n_new, _ = x_sample.shape
    depth = norm_ffn1.shape[0]
    h_fox = b_forget.shape[1]
    h_moba = w_branch_moba.shape[1] // HEAD_DIM
    h_sb = w_branch_sb.shape[1] // HEAD_DIM
    heads = (h_moba, h_sb, h_fox)
    h_tot = sum(heads)
    w_mix = h_tot * HEAD_DIM
    past_len = page_table.shape[1] * cache_k.shape[2]
    assert past_len % MOBA_BLOCK == 0 and n_new <= MOBA_BLOCK
    assert (3 * w_mix + 3 * d) % PROJ_TN == 0 and h_fox <= LANES

    d_ff = ffn1_w_gate.shape[2]
    ff_pad = -d_ff % 512

    def up(wt):
        return jnp.pad(wt.astype(BF16), ((0, 0), (0, 0), (0, ff_pad)))

    def down(wt):
        return jnp.pad(wt.astype(BF16), ((0, 0), (0, ff_pad), (0, 0)))

    f1g, f1u, f1d = up(ffn1_w_gate), up(ffn1_w_up), down(ffn1_w_down)
    f2g, f2u, f2d = up(ffn2_w_gate), up(ffn2_w_up), down(ffn2_w_down)
    w_in_b = w_in.astype(BF16)
    wba, wbb, wbc, wo = (a.astype(BF16) for a in (w_branch_moba, w_branch_sb, w_branch_fox, w_out))
    b_pad = jnp.pad(b_forget, ((0, 0), (0, LANES - h_fox)))

    cos_p, sin_p = _rope_tables(jnp.arange(t, dtype=jnp.int32))
    cos_s, sin_s = _rope_tables(past_len + jnp.arange(n_new, dtype=jnp.int32))
    cos_s, sin_s = jnp.tile(cos_s, (db, 1)), jnp.tile(sin_s, (db, 1))

    yp = x_prompt.reshape(batch * t, d)
    ys = x_sample.reshape(db * n_new, d)
    outs = [[] for _ in range(6)]
    for l in range(depth):
        w = dict(g_f1=norm_ffn1[l], f1_gate=f1g[l], f1_up=f1u[l], f1_down=f1d[l], g_mix=norm_mix[l],
                 w_in=w_in_b, layer=l, b_f=b_pad[l:l + 1],
                 q_gains=jnp.stack([q_norm_moba[l], q_norm_fox[l]]).reshape(2, 1, HEAD_DIM),
                 k_gains=jnp.stack([k_norm_moba[l], k_norm_fox[l]]).reshape(2, 1, HEAD_DIM),
                 wb_a=wba[l], wb_b=wbb[l], wb_c=wbc[l], w_out=wo[l],
                 g_f2=norm_ffn2[l], f2_gate=f2g[l], f2_up=f2u[l], f2_down=f2d[l])

        h, q_bf, k_bf, k_f32, v_bf, v_f32, gates, logf = _layer_common(yp, w, heads, cos_p, sin_p)
        fcol, frow = _cumsum_prompt(logf, batch, t)
        o_a, o_b, o_c = _prompt_attention(q_bf, k_bf, v_bf, fcol, frow, batch, t, heads)
        yp = _layer_tail(h, o_a, o_b, o_c, gates, w)
        outs[0].append(k_f32.reshape(batch, t, h_tot, HEAD_DIM))
        outs[1].append(v_f32.reshape(batch, t, h_tot, HEAD_DIM))
        outs[2].append(logf[:, :h_fox].reshape(batch, t, h_fox))

        h, q_bf, k_bf, k_f32, v_bf, v_f32, gates, logf = _layer_common(ys, w, heads, cos_s, sin_s)
        fcol_past = _cumsum_pages(cache_logf, l, page_table)
        o = _sample_attention(q_bf.astype(F32), k_f32, v_f32, logf, fcol_past, cache_k, cache_v, l,
                              page_table, heads)
        a_end, b_end = h_moba * HEAD_DIM, (h_moba + h_sb) * HEAD_DIM
        ys = _layer_tail(h, o[:, :a_end], o[:, a_end:b_end], o[:, b_end:], gates, w)
        outs[3].append(k_f32.reshape(db, n_new, h_tot, HEAD_DIM))
        outs[4].append(v_f32.reshape(db, n_new, h_tot, HEAD_DIM))
        outs[5].append(logf[:, :h_fox].reshape(db, n_new, h_fox))

    return (yp.reshape(batch, t, d), ys.reshape(db, n_new, d)) + tuple(jnp.stack(o) for o in outs)
```

```python
import functools

import jax
import jax.numpy as jnp
from jax import lax
from jax.experimental import pallas as pl
from jax.experimental.pallas import tpu as pltpu

F32 = jnp.float32
BF16 = jnp.bfloat16

LANES = 128
SUBLANES = 8
VMEM_LIMIT_BYTES = 56 * 1024 * 1024

HEAD_DIM = 128
MOBA_BLOCK = 256
MOBA_TOPK = 3
ROPE_THETA = 10000.0
NORM_EPS = 1e-6
Q_TILE = 256
HEAD_GROUP = SUBLANES

_NT = (((1,), (1,)), ((), ()))
_HI = lax.Precision.HIGHEST


def _cparams(*sem):
    return pltpu.CompilerParams(dimension_semantics=sem, vmem_limit_bytes=VMEM_LIMIT_BYTES)


def _dot(a, b):
    return jnp.dot(a, b, preferred_element_type=F32)


def _dot_nt(a, b):
    return lax.dot_general(a, b, _NT, preferred_element_type=F32)


def _split3(x):
    hi = x.astype(BF16)
    r = x - hi.astype(F32)
    mid = r.astype(BF16)
    lo = (r - mid.astype(F32)).astype(BF16)
    return hi, mid, lo


def _tri_dot_left(tri, x):
    hi, mid, lo = _split3(x)
    return _dot(tri, hi) + _dot(tri, mid) + _dot(tri, lo)


def _tri_dot_right(x, tri):
    hi, mid, lo = _split3(x)
    return _dot(hi, tri) + _dot(mid, tri) + _dot(lo, tri)


def _log_sigmoid_pair(z):
    sp = jnp.log(1.0 + jnp.exp(-jnp.abs(z)))
    return jnp.minimum(z, 0.0) - sp, jnp.minimum(-z, 0.0) - sp


def _tile(m, pref):
    t = pref
    while m % t:
        t //= 2
    assert t >= 1 and (t % SUBLANES == 0 or t == m), (m, pref)
    return min(t, m)


def _rmsnorm_body(x_ref, g_ref, o_ref):
    x = x_ref[...]
    ms = jnp.mean(x * x, axis=-1, keepdims=True)
    o_ref[...] = (x * lax.rsqrt(ms + NORM_EPS) * g_ref[...]).astype(o_ref.dtype)


def _rmsnorm(x, g):
    m, d = x.shape
    tm = _tile(m, 512)
    return pl.pallas_call(
        _rmsnorm_body,
        grid=(m // tm,),
        in_specs=[pl.BlockSpec((tm, d), lambda i: (i, 0)), pl.BlockSpec((1, d), lambda i: (0, 0))],
        out_specs=pl.BlockSpec((tm, d), lambda i: (i, 0)),
        out_shape=jax.ShapeDtypeStruct((m, d), BF16),
        compiler_params=_cparams("parallel"),
        name="rmsnorm",
    )(x, g.reshape(1, d))


def _ffn_up_body(x_ref, wg_ref, wu_ref, o_ref):
    x = x_ref[...]
    g = _dot(x, wg_ref[...])
    u = _dot(x, wu_ref[...])
    o_ref[...] = (g * jax.nn.sigmoid(g) * u).astype(o_ref.dtype)


def _ffn_up(xn, wg, wu):
    m, k = xn.shape
    n = wg.shape[1]
    tm, tn = _tile(m, 1024), _tile(n, 512)
    return pl.pallas_call(
        _ffn_up_body,
        grid=(m // tm, n // tn),
        in_specs=[pl.BlockSpec((tm, k), lambda i, j: (i, 0)),
                  pl.BlockSpec((k, tn), lambda i, j: (0, j)),
                  pl.BlockSpec((k, tn), lambda i, j: (0, j))],
        out_specs=pl.BlockSpec((tm, tn), lambda i, j: (i, j)),
        out_shape=jax.ShapeDtypeStruct((m, n), BF16),
        compiler_params=_cparams("parallel", "arbitrary"),
        name="ffn_up",
    )(xn, wg, wu)


def _mm_resid_body(x_ref, w_ref, r_ref, o_ref, *, scale):
    o_ref[...] = r_ref[...] + scale * _dot(x_ref[...], w_ref[...])


def _mm_resid(x, w, resid, scale):
    m, k = x.shape
    n = w.shape[1]
    tm, tn = _tile(m, 1024), _tile(n, 512)
    return pl.pallas_call(
        functools.partial(_mm_resid_body, scale=scale),
        grid=(m // tm, n // tn),
        in_specs=[pl.BlockSpec((tm, k), lambda i, j: (i, 0)),
                  pl.BlockSpec((k, tn), lambda i, j: (0, j)),
                  pl.BlockSpec((tm, tn), lambda i, j: (i, j))],
        out_specs=pl.BlockSpec((tm, tn), lambda i, j: (i, j)),
        out_shape=jax.ShapeDtypeStruct((m, n), F32),
        compiler_params=_cparams("parallel", "arbitrary"),
        name="mm_resid",
    )(x, w, resid)


PROJ_TN = 2 * HEAD_DIM


def _w_in_spec(w_in, layer, col_of):
    return pl.BlockSpec((None, w_in.shape[1], PROJ_TN), lambda *ids: (layer, 0, col_of(*ids)))


def _proj_qk_body(x_ref, w_ref, g_ref, cos_ref, sin_ref, *out_refs, n_rope, n_plain_end):
    j = pl.program_id(1)
    acc = _dot(x_ref[...], w_ref[...])

    def write(val):
        out_refs[0][...] = val.astype(BF16)
        if len(out_refs) > 1:
            out_refs[1][...] = val

    def normed(c):
        a = acc[:, c * HEAD_DIM:(c + 1) * HEAD_DIM]
        ms = jnp.mean(a * a, axis=-1, keepdims=True)
        return a * lax.rsqrt(ms + NORM_EPS) * g_ref[0]

    n_heads_tile = PROJ_TN // HEAD_DIM

    @pl.when(j < n_rope)
    def _():
        cos, sin = cos_ref[...], sin_ref[...]
        outs = []
        for c in range(n_heads_tile):
            a = normed(c)
            outs.append(a * cos + pltpu.roll(a, HEAD_DIM // 2, 1) * sin)
        write(jnp.concatenate(outs, axis=-1))

    @pl.when(jnp.logical_and(j >= n_rope, j < n_plain_end))
    def _():
        write(acc)

    @pl.when(j >= n_plain_end)
    def _():
        write(jnp.concatenate([normed(c) for c in range(n_heads_tile)], axis=-1))


def _proj_qk(xn, w_in, layer, col_tile0, gains, cos_t, sin_t, heads, emit_f32):
    m, k = xn.shape
    h_moba, h_sb, h_fox = heads
    w_mix = (h_moba + h_sb + h_fox) * HEAD_DIM
    per = PROJ_TN // HEAD_DIM
    assert h_moba % per == 0 and h_sb % per == 0 and h_fox % per == 0
    n_rope, n_plain_end, n_tiles = h_moba // per, (h_moba + h_sb) // per, w_mix // PROJ_TN
    assert m % cos_t.shape[0] == 0
    tm = _tile(cos_t.shape[0], 1024)
    n_tbl = cos_t.shape[0] // tm
    out_shape = [jax.ShapeDtypeStruct((m, w_mix), BF16)]
    out_specs = [pl.BlockSpec((tm, PROJ_TN), lambda i, j: (i, j))]
    if emit_f32:
        out_shape.append(jax.ShapeDtypeStruct((m, w_mix), F32))
        out_specs.append(pl.BlockSpec((tm, PROJ_TN), lambda i, j: (i, j)))
    return pl.pallas_call(
        functools.partial(_proj_qk_body, n_rope=n_rope, n_plain_end=n_plain_end),
        grid=(m // tm, n_tiles),
        in_specs=[pl.BlockSpec((tm, k), lambda i, j: (i, 0)),
                  _w_in_spec(w_in, layer, lambda i, j: col_tile0 + j),
                  pl.BlockSpec((None, 1, HEAD_DIM), lambda i, j: (jnp.where(j >= n_plain_end, 1, 0), 0, 0)),
                  pl.BlockSpec((tm, HEAD_DIM), lambda i, j: (i % n_tbl, 0)),
                  pl.BlockSpec((tm, HEAD_DIM), lambda i, j: (i % n_tbl, 0))],
        out_specs=out_specs,
        out_shape=out_shape,
        compiler_params=_cparams("parallel", "arbitrary"),
        name="proj_qk",
    )(xn, w_in, gains, cos_t, sin_t)


def _proj_v_body(x_ref, w_ref, o_bf_ref, o_f32_ref):
    acc = _dot(x_ref[...], w_ref[...])
    o_bf_ref[...] = acc.astype(BF16)
    o_f32_ref[...] = acc


def _proj_v(xn, w_in, layer, col_tile0, n_cols):
    m, k = xn.shape
    tm = _tile(m, 1024)
    spec = pl.BlockSpec((tm, PROJ_TN), lambda i, j: (i, j))
    return pl.pallas_call(
        _proj_v_body,
        grid=(m // tm, n_cols // PROJ_TN),
        in_specs=[pl.BlockSpec((tm, k), lambda i, j: (i, 0)),
                  _w_in_spec(w_in, layer, lambda i, j: col_tile0 + j)],
        out_specs=[spec, spec],
        out_shape=[jax.ShapeDtypeStruct((m, n_cols), BF16), jax.ShapeDtypeStruct((m, n_cols), F32)],
        compiler_params=_cparams("parallel", "arbitrary"),
        name="proj_v",
    )(xn, w_in)


def _proj_gate_body(x_ref, w_ref, o_ref):
    o_ref[...] = jax.nn.sigmoid(_dot(x_ref[...], w_ref[...]))


def _proj_gate(xn, w_in, layer, col_tile0, n_cols):
    m, k = xn.shape
    tm = _tile(m, 1024)
    return pl.pallas_call(
        _proj_gate_body,
        grid=(m // tm, n_cols // PROJ_TN),
        in_specs=[pl.BlockSpec((tm, k), lambda i, j: (i, 0)),
                  _w_in_spec(w_in, layer, lambda i, j: col_tile0 + j)],
        out_specs=pl.BlockSpec((tm, PROJ_TN), lambda i, j: (i, j)),
        out_shape=jax.ShapeDtypeStruct((m, n_cols), F32),
        compiler_params=_cparams("parallel", "arbitrary"),
        name="proj_gate",
    )(xn, w_in)


def _proj_logf_body(x_ref, w_ref, b_ref, o_ref, *, n_valid):
    acc = _dot(x_ref[...], w_ref[...])[:, :LANES]
    lane = lax.broadcasted_iota(jnp.int32, acc.shape, 1)
    ls, _ = _log_sigmoid_pair(acc + b_ref[...])
    o_ref[...] = jnp.where(lane < n_valid, ls, 0.0)


def _proj_logf(xn, w_in, layer, col_tile0, b_pad, n_valid):
    m, k = xn.shape
    tm = _tile(m, 1024)
    return pl.pallas_call(
        functools.partial(_proj_logf_body, n_valid=n_valid),
        grid=(m // tm,),
        in_specs=[pl.BlockSpec((tm, k), lambda i: (i, 0)),
                  _w_in_spec(w_in, layer, lambda i: col_tile0),
                  pl.BlockSpec((1, LANES), lambda i: (0, 0))],
        out_specs=pl.BlockSpec((tm, LANES), lambda i: (i, 0)),
        out_shape=jax.ShapeDtypeStruct((m, LANES), F32),
        compiler_params=_cparams("parallel"),
        name="proj_logf",
    )(xn, w_in, b_pad)


def _merge_body(oa_ref, ob_ref, oc_ref, wa_ref, wb_ref, wc_ref, ga_ref, gb_ref, gc_ref, o_ref):
    merged = (ga_ref[...] * _dot(oa_ref[...].astype(BF16), wa_ref[...])
              + gb_ref[...] * _dot(ob_ref[...].astype(BF16), wb_ref[...])
              + gc_ref[...] * _dot(oc_ref[...].astype(BF16), wc_ref[...]))
    o_ref[...] = merged.astype(o_ref.dtype)


def _merge(o_a, o_b, o_c, wb_a, wb_b, wb_c, gates):
    m = o_a.shape[0]
    d = wb_a.shape[1]
    tm, tn = _tile(m, 1024), _tile(d, 512)
    nd = d // tn

    def o_spec(o):
        return pl.BlockSpec((tm, o.shape[1]), lambda i, j: (i, 0))

    def w_spec(w):
        return pl.BlockSpec((w.shape[0], tn), lambda i, j: (0, j))

    def g_spec(b):
        return pl.BlockSpec((tm, tn), lambda i, j: (i, b * nd + j))

    return pl.pallas_call(
        _merge_body,
        grid=(m // tm, nd),
        in_specs=[o_spec(o_a), o_spec(o_b), o_spec(o_c), w_spec(wb_a), w_spec(wb_b), w_spec(wb_c),
                  g_spec(0), g_spec(1), g_spec(2)],
        out_specs=pl.BlockSpec((tm, tn), lambda i, j: (i, j)),
        out_shape=jax.ShapeDtypeStruct((m, d), BF16),
        compiler_params=_cparams("parallel", "arbitrary"),
        name="merge",
    )(o_a, o_b, o_c, wb_a, wb_b, wb_c, gates, gates, gates)


def _rope_table_body(pos_ref, inv_ref, cos_ref, sin_ref):
    ang = pos_ref[...] * inv_ref[...]
    lane = lax.broadcasted_iota(jnp.int32, ang.shape, 1)
    cos_ref[...] = jnp.cos(ang)
    sin_ref[...] = jnp.where(lane < HEAD_DIM // 2, -1.0, 1.0) * jnp.sin(ang)


def _rope_tables(pos):
    half = HEAD_DIM // 2
    inv = ROPE_THETA ** (-jnp.arange(half, dtype=F32) / half)
    inv_full = jnp.concatenate([inv, inv]).reshape(1, HEAD_DIM)
    r = pos.shape[0]
    spec = pl.BlockSpec((r, HEAD_DIM), lambda: (0, 0))
    return pl.pallas_call(
        _rope_table_body,
        in_specs=[pl.BlockSpec((r, 1), lambda: (0, 0)), pl.BlockSpec((1, HEAD_DIM), lambda: (0, 0))],
        out_specs=[spec, spec],
        out_shape=[jax.ShapeDtypeStruct((r, HEAD_DIM), F32)] * 2,
        name="rope_tables",
    )(pos.astype(F32).reshape(r, 1), inv_full)


def _tri(n, kind):
    r = lax.broadcasted_iota(jnp.int32, (n, n), 0)
    c = lax.broadcasted_iota(jnp.int32, (n, n), 1)
    return {"row_ge_col": r >= c, "row_gt_col": r > c}[kind].astype(BF16)


def _cumsum_prompt_body(x_ref, tri_ref, fcol_ref, frow_ref, *, n_chunks):
    tri = tri_ref[...]
    c_len = tri.shape[0]
    carry = jnp.zeros((1, LANES), F32)
    for c in range(n_chunks):
        f = _tri_dot_left(tri, x_ref[c * c_len:(c + 1) * c_len, :]) + carry
        fcol_ref[c * c_len:(c + 1) * c_len, :] = f
        frow_ref[:, c * c_len:(c + 1) * c_len] = f.T
        carry = f[c_len - 1:c_len, :]


def _cumsum_prompt(logf, batch, t):
    c_len = MOBA_BLOCK
    return pl.pallas_call(
        functools.partial(_cumsum_prompt_body, n_chunks=t // c_len),
        grid=(batch,),
        in_specs=[pl.BlockSpec((t, LANES), lambda b: (b, 0)),
                  pl.BlockSpec((c_len, c_len), lambda b: (0, 0))],
        out_specs=[pl.BlockSpec((t, LANES), lambda b: (b, 0)),
                   pl.BlockSpec((None, LANES, t), lambda b: (b, 0, 0))],
        out_shape=[jax.ShapeDtypeStruct((batch * t, LANES), F32),
                   jax.ShapeDtypeStruct((batch, LANES, t), F32)],
        compiler_params=_cparams("parallel"),
        name="cumsum_prompt",
    )(logf, _tri(c_len, "row_ge_col"))


CUMSUM_PAGES_PER_STEP = 8


def _cumsum_pages_body(pt_ref, *refs, n_heads, n_in):
    x_refs = refs[:n_in]
    tri_ref, f_ref, pad_ref, carry_ref = refs[n_in:]
    page = tri_ref.shape[0]

    @pl.when(pl.program_id(1) == 0)
    def _():
        carry_ref[...] = jnp.zeros_like(carry_ref)

    pad_ref[...] = jnp.zeros_like(pad_ref)
    carry = carry_ref[...]
    for e in range(n_in):
        pad_ref[:, :n_heads] = x_refs[e][...]
        f = _tri_dot_left(tri_ref[...], pad_ref[...]) + carry
        f_ref[e * page:(e + 1) * page, :] = f
        carry = f[page - 1:, :]
    carry_ref[...] = carry


def _cumsum_pages(cache_logf, layer, page_table):
    _, _, page, n_heads = cache_logf.shape
    db, n_pages = page_table.shape
    assert page == LANES
    n_in = CUMSUM_PAGES_PER_STEP
    while n_pages % n_in:
        n_in //= 2

    def page_spec(e):
        return pl.BlockSpec((None, None, page, n_heads), lambda b, p, pt: (layer, pt[b, p * n_in + e], 0, 0))

    grid_spec = pltpu.PrefetchScalarGridSpec(
        num_scalar_prefetch=1,
        grid=(db, n_pages // n_in),
        in_specs=[page_spec(e) for e in range(n_in)] + [pl.BlockSpec((page, page), lambda b, p, pt: (0, 0))],
        out_specs=pl.BlockSpec((None, n_in * page, LANES), lambda b, p, pt: (b, p, 0)),
        scratch_shapes=[pltpu.VMEM((page, LANES), F32), pltpu.VMEM((1, LANES), F32)],
    )
    return pl.pallas_call(
        functools.partial(_cumsum_pages_body, n_heads=n_heads, n_in=n_in),
        grid_spec=grid_spec,
        out_shape=jax.ShapeDtypeStruct((db, n_pages * page, LANES), F32),
        compiler_params=_cparams("parallel", "arbitrary"),
        name="cumsum_pages",
    )(page_table, *([cache_logf] * n_in), _tri(page, "row_ge_col"))


HEADS_PER_STEP = 2


def _head_lanes(e):
    return slice(e * HEAD_DIM, (e + 1) * HEAD_DIM)


def _causal_iotas(n):
    row = lax.broadcasted_iota(jnp.int32, (n, n), 0)
    col = lax.broadcasted_iota(jnp.int32, (n, n), 1)
    return row, col


def _moba_pair_shift(nb):
    shift = max(nb - 1, 1).bit_length()
    assert (1 << shift) ** 2 <= LANES
    return shift


def _moba_pair_matrices(nb):
    shift = _moba_pair_shift(nb)
    width = 1 << shift
    lane = jnp.arange(LANES, dtype=jnp.int32)
    hi, lo, ok = lane >> shift, lane & (width - 1), lane < width * width
    blk_id = jnp.arange(LANES, dtype=jnp.int32)[None, :]
    pair_t = jnp.where(ok[:, None], (hi[:, None] == blk_id).astype(F32) - (lo[:, None] == blk_id).astype(F32), 0.0)
    sum_mat = jnp.logical_and(ok[:, None], hi[:, None] == blk_id).astype(BF16)
    return pair_t, sum_mat


def _moba_select(q, kdiff, sum_mat, i, nb):
    shift = _moba_pair_shift(nb)
    width = 1 << shift
    diff = lax.dot_general(q.astype(F32), kdiff, _NT, precision=_HI, preferred_element_type=F32)
    lane = lax.broadcasted_iota(jnp.int32, diff.shape, 1)
    n_l, m_l = lax.shift_right_logical(lane, shift), jnp.bitwise_and(lane, width - 1)
    beats = jnp.logical_or(diff < 0.0, jnp.logical_and(diff == 0.0, m_l < n_l))
    beats = jnp.logical_and(beats, jnp.logical_and(m_l < i, lane < width * width))
    rank = _dot(jnp.where(beats, 1.0, 0.0).astype(BF16), sum_mat)
    return jnp.where(jnp.logical_and(rank < MOBA_TOPK, lane < i), 1.0, 0.0)


def _moba_prompt_body(q_ref, k_ref, v_ref, pair_ref, sum_ref, o_ref, kmean_ref, kdiff_ref, *, nb, scale):
    i = pl.program_id(2)
    blk = MOBA_BLOCK

    @pl.when(i == 0)
    def _():
        kmean_ref[...] = jnp.zeros_like(kmean_ref)
        for e in range(HEADS_PER_STEP):
            for n in range(nb):
                kmean_ref[e, n:n + 1, :] = jnp.mean(k_ref[n * blk:(n + 1) * blk, _head_lanes(e)].astype(F32),
                                                    axis=0, keepdims=True)
            kdiff_ref[e] = jnp.dot(pair_ref[...], kmean_ref[e], precision=_HI, preferred_element_type=F32)

    start = pl.multiple_of(i * blk, blk)
    row, col = _causal_iotas(blk)
    causal = col <= row
    qs = [q_ref[:, _head_lanes(e)] for e in range(HEADS_PER_STEP)]
    sum_mat = sum_ref[...]
    sels = [_moba_select(qs[e], kdiff_ref[e], sum_mat, i, nb) for e in range(HEADS_PER_STEP)]
    lane = lax.broadcasted_iota(jnp.int32, sels[0].shape, 1)

    def own(e):
        s = jnp.where(causal, _dot_nt(qs[e], k_ref[pl.ds(start, blk), _head_lanes(e)]) * scale, -jnp.inf)
        m0 = jnp.max(s, axis=-1, keepdims=True)
        p = jnp.exp(s - m0)
        return m0, jnp.sum(p, axis=-1, keepdims=True), _dot(p.astype(BF16), v_ref[pl.ds(start, blk), _head_lanes(e)])

    def past(n, carry):
        st = pl.multiple_of(n * blk, blk)
        raws = [_dot_nt(qs[e], k_ref[pl.ds(st, blk), _head_lanes(e)]) for e in range(HEADS_PER_STEP)]
        out = []
        for e in range(HEADS_PER_STEP):
            m, l, acc = carry[e]
            chosen = jnp.sum(jnp.where(lane == n, sels[e], 0.0), axis=-1, keepdims=True) > 0.5
            s = jnp.where(chosen, raws[e] * scale, -jnp.inf)
            m_new = jnp.maximum(m, jnp.max(s, axis=-1, keepdims=True))
            alpha = jnp.exp(m - m_new)
            p = jnp.exp(s - m_new)
            l = alpha * l + jnp.sum(p, axis=-1, keepdims=True)
            acc = alpha * acc + _dot(p.astype(BF16), v_ref[pl.ds(st, blk), _head_lanes(e)])
            out.append((m_new, l, acc))
        return tuple(out)

    fin = lax.fori_loop(0, i, past, tuple(own(e) for e in range(HEADS_PER_STEP)))
    o_ref[...] = jnp.concatenate([acc / l for _, l, acc in fin], axis=-1).astype(o_ref.dtype)


def _sb_prompt_body(q_ref, k_ref, v_ref, tri_ref, o_ref, *, scale):
    i = pl.program_id(2)
    blk = Q_TILE
    qs = [q_ref[:, _head_lanes(e)] for e in range(HEADS_PER_STEP)]
    tri = tri_ref[...]

    def logits(e, st):
        return _dot_nt(qs[e], k_ref[pl.ds(st, blk), _head_lanes(e)]) * scale

    def block(e, st, z, carry, acc, strict):
        ls, lsn = _log_sigmoid_pair(z)
        if strict is not None:
            lsn = jnp.where(strict, lsn, 0.0)
        hi = lsn.astype(BF16)
        lo = (lsn - hi.astype(F32)).astype(BF16)
        later = _dot(hi, tri) + _dot(lo, tri)
        w = jnp.exp(ls + later + carry)
        if strict is not None:
            w = jnp.where(strict, w, 0.0)
        acc = acc + _dot(w.astype(BF16), v_ref[pl.ds(st, blk), _head_lanes(e)])
        return carry + jnp.sum(lsn, axis=-1, keepdims=True), acc

    row, col = _causal_iotas(blk)
    start = pl.multiple_of(i * blk, blk)
    zs = [logits(e, start) for e in range(HEADS_PER_STEP)]
    init = tuple(block(e, start, zs[e], jnp.zeros((blk, 1), F32),
                       jnp.zeros((blk, HEAD_DIM), F32), col < row) for e in range(HEADS_PER_STEP))

    def past(t, c):
        st = pl.multiple_of((i - 1 - t) * blk, blk)
        zs = [logits(e, st) for e in range(HEADS_PER_STEP)]
        return tuple(block(e, st, zs[e], c[e][0], c[e][1], None) for e in range(HEADS_PER_STEP))

    fin = lax.fori_loop(0, i, past, init)
    o_ref[...] = jnp.concatenate([acc for _, acc in fin], axis=-1).astype(o_ref.dtype)


def _fox_prompt_body(q_ref, k_ref, v_ref, fcol_ref, frow_ref, o_ref, *, scale):
    h = pl.program_id(1)
    i = pl.program_id(2)
    blk = Q_TILE
    qs = [q_ref[:, _head_lanes(e)] for e in range(HEADS_PER_STEP)]
    fcol = fcol_ref[...]
    lane = lax.broadcasted_iota(jnp.int32, fcol.shape, 1)
    fqs = [jnp.sum(jnp.where(lane == h * HEADS_PER_STEP + e, fcol, 0.0), axis=-1, keepdims=True)
           for e in range(HEADS_PER_STEP)]

    def scores(e, st):
        fk = frow_ref[pl.ds(h * HEADS_PER_STEP + e, 1), pl.ds(st, blk)]
        return _dot_nt(qs[e], k_ref[pl.ds(st, blk), _head_lanes(e)]) * scale + (fqs[e] - fk)

    start = pl.multiple_of(i * blk, blk)
    row, col = _causal_iotas(blk)
    causal = col <= row

    def own(e):
        s = jnp.where(causal, scores(e, start), -jnp.inf)
        m0 = jnp.max(s, axis=-1, keepdims=True)
        p = jnp.exp(s - m0)
        return m0, jnp.sum(p, axis=-1, keepdims=True), _dot(p.astype(BF16), v_ref[pl.ds(start, blk), _head_lanes(e)])

    def past(n, carry):
        st = pl.multiple_of(n * blk, blk)
        ss = [scores(e, st) for e in range(HEADS_PER_STEP)]
        out = []
        for e in range(HEADS_PER_STEP):
            m, l, acc = carry[e]
            s = ss[e]
            m_new = jnp.maximum(m, jnp.max(s, axis=-1, keepdims=True))
            alpha = jnp.exp(m - m_new)
            p = jnp.exp(s - m_new)
            l = alpha * l + jnp.sum(p, axis=-1, keepdims=True)
            acc = alpha * acc + _dot(p.astype(BF16), v_ref[pl.ds(st, blk), _head_lanes(e)])
            out.append((m_new, l, acc))
        return tuple(out)

    fin = lax.fori_loop(0, i, past, tuple(own(e) for e in range(HEADS_PER_STEP)))
    o_ref[...] = jnp.concatenate([acc / l for _, l, acc in fin], axis=-1).astype(o_ref.dtype)


def _prompt_attention(q_bf, k_bf, v_bf, fcol, frow, batch, t, heads):
    h_moba, h_sb, h_fox = heads
    assert t % Q_TILE == 0 and Q_TILE == MOBA_BLOCK
    nq = t // Q_TILE
    m = batch * t
    scale = HEAD_DIM ** -0.5
    hps = HEADS_PER_STEP
    assert h_moba % hps == 0 and h_sb % hps == 0 and h_fox % hps == 0
    width = hps * HEAD_DIM

    def specs(h0):
        qs = pl.BlockSpec((Q_TILE, width), lambda b, h, i: (b * nq + i, h0 // hps + h))
        ks = pl.BlockSpec((t, width), lambda b, h, i: (b, h0 // hps + h))
        return qs, ks, ks

    def out(n_heads):
        return (pl.BlockSpec((Q_TILE, width), lambda b, h, i: (b * nq + i, h)),
                jax.ShapeDtypeStruct((m, n_heads * HEAD_DIM), BF16))

    sem = _cparams("parallel", "parallel", "arbitrary")

    o_spec, o_shape = out(h_moba)
    o_a = pl.pallas_call(
        functools.partial(_moba_prompt_body, nb=nq, scale=scale),
        grid=(batch, h_moba // hps, nq),
        in_specs=list(specs(0)) + [pl.BlockSpec((LANES, LANES), lambda b, h, i: (0, 0))] * 2,
        out_specs=o_spec, out_shape=o_shape,
        scratch_shapes=[pltpu.VMEM((hps, LANES, HEAD_DIM), F32), pltpu.VMEM((hps, LANES, HEAD_DIM), F32)],
        compiler_params=sem, name="moba_prompt",
    )(q_bf, k_bf, v_bf, *_moba_pair_matrices(nq))

    o_spec, o_shape = out(h_sb)
    o_b = pl.pallas_call(
        functools.partial(_sb_prompt_body, scale=scale),
        grid=(batch, h_sb // hps, nq),
        in_specs=list(specs(h_moba)) + [pl.BlockSpec((Q_TILE, Q_TILE), lambda b, h, i: (0, 0))],
        out_specs=o_spec, out_shape=o_shape,
        compiler_params=sem, name="sb_prompt",
    )(q_bf, k_bf, v_bf, _tri(Q_TILE, "row_gt_col"))

    o_spec, o_shape = out(h_fox)
    o_c = pl.pallas_call(
        functools.partial(_fox_prompt_body, scale=scale),
        grid=(batch, h_fox // hps, nq),
        in_specs=list(specs(h_moba + h_sb)) + [
            pl.BlockSpec((Q_TILE, LANES), lambda b, h, i: (b * nq + i, 0)),
            pl.BlockSpec((None, LANES, t), lambda b, h, i: (b, 0, 0))],
        out_specs=o_spec, out_shape=o_shape,
        compiler_params=sem, name="fox_prompt",
    )(q_bf, k_bf, v_bf, fcol, frow)
    return o_a, o_b, o_c


def _sample_attn_body(pt_ref, q_ref, kn_ref, vn_ref, lfn_ref, fexp_ref, ftot_ref, tri_ref, utile_ref, *refs,
                      heads, n_groups, n_pages, scale, n_new):
    h_moba, h_sb, h_fox = heads
    h_tot = h_moba + h_sb + h_fox
    (k_ref, v_ref, o_ref, kpad_ref, vpad_ref, acc_ref, m_ref, l_ref, carry_ref, fq_ref, cumn_t_ref,
     pm_ref, pl_ref, po_ref, pg_ref, gsb_ref) = refs

    r = pl.program_id(1)
    page = LANES
    rows = q_ref.shape[0]
    pages_per_block = MOBA_BLOCK // page
    grp_rows = HEAD_GROUP * rows
    grp_keys = HEAD_GROUP * page

    def q_head(h):
        return q_ref[:, h * HEAD_DIM:(h + 1) * HEAD_DIM].astype(BF16)

    def bcast(x):
        return jnp.broadcast_to(x, (x.shape[0], LANES))

    def head_rows(h):
        return slice(h * rows, (h + 1) * rows)

    @pl.when(r == 0)
    def _():
        kpad_ref[...] = jnp.zeros_like(kpad_ref)
        vpad_ref[...] = jnp.zeros_like(vpad_ref)
        kpad_ref[0:n_new, :] = kn_ref[...]
        vpad_ref[0:n_new, :] = vn_ref[...]
        row = lax.broadcasted_iota(jnp.int32, (rows, page), 0)
        col = lax.broadcasted_iota(jnp.int32, (rows, page), 1)
        weak = col <= row
        strict = col < row

        cumn_t_ref[...] = jnp.zeros_like(cumn_t_ref)
        cumn_t_ref[0:rows, :] = lfn_ref[...]
        lfn_pad = cumn_t_ref[...]
        cum_pad = _tri_dot_left(tri_ref[...], lfn_pad) + lfn_pad
        cumn = cum_pad[0:rows, :]
        cumn_t_ref[...] = cum_pad.T

        for h in range(h_tot):
            hr = head_rows(h)
            kh = kpad_ref[:, h * HEAD_DIM:(h + 1) * HEAD_DIM].astype(BF16)
            vh = vpad_ref[:, h * HEAD_DIM:(h + 1) * HEAD_DIM].astype(BF16)
            s_raw = _dot_nt(q_head(h), kh)
            if h < h_moba:
                s = jnp.where(weak, s_raw * scale, -jnp.inf)
                m = jnp.max(s, axis=-1, keepdims=True)
                p = jnp.exp(s - m)
                pm_ref[n_pages, hr] = bcast(m)
                pl_ref[n_pages, hr] = bcast(jnp.sum(p, axis=-1, keepdims=True))
                po_ref[n_pages, hr] = _dot(p.astype(BF16), vh)
            elif h < h_moba + h_sb:
                ls, lsn = _log_sigmoid_pair(s_raw * scale)
                lsn = jnp.where(strict, lsn, 0.0)
                later = _tri_dot_right(lsn, tri_ref[...])
                w = jnp.where(strict, jnp.exp(ls + later), 0.0)
                acc_ref[hr] = _dot(w.astype(BF16), vh)
                carry_ref[hr] = bcast(jnp.sum(lsn, axis=-1, keepdims=True))
            else:
                c = h - h_moba - h_sb
                bias = cumn[:, c:c + 1] - cumn_t_ref[c:c + 1, :]
                s = jnp.where(weak, s_raw * scale + bias, -jnp.inf)
                m = jnp.max(s, axis=-1, keepdims=True)
                p = jnp.exp(s - m)
                m_ref[hr] = bcast(m)
                l_ref[hr] = bcast(jnp.sum(p, axis=-1, keepdims=True))
                acc_ref[hr] = _dot(p.astype(BF16), vh)
                fq_ref[hr] = bcast(cumn[:, c:c + 1] + ftot_ref[0:1, c:c + 1])

    lane = lax.broadcasted_iota(jnp.int32, (grp_rows, grp_keys), 1)
    qrow = lax.broadcasted_iota(jnp.int32, (grp_rows, grp_keys), 0)
    own = jnp.bitwise_and(lane, HEAD_GROUP - 1) == lax.shift_right_logical(qrow, 3)
    assert rows == 8 and HEAD_GROUP == 8
    n_tiles = grp_keys // LANES

    raws = []
    for g in range(n_groups):
        h0 = g * HEAD_GROUP
        qg = jnp.concatenate([q_ref[:, (h0 + hh) * HEAD_DIM:(h0 + hh + 1) * HEAD_DIM]
                              for hh in range(HEAD_GROUP)], axis=0).astype(BF16)
        kb = k_ref[:, h0:h0 + HEAD_GROUP, :].reshape(grp_keys, HEAD_DIM).astype(BF16)
        raws.append(_dot_nt(qg, kb))

    for g in range(n_groups):
        h0 = g * HEAD_GROUP
        n_m = min(max(h_moba - h0, 0), HEAD_GROUP)
        n_s = min(max(h_moba + h_sb - h0, 0), HEAD_GROUP) - n_m
        r_m, r_s = n_m * rows, (n_m + n_s) * rows
        base = h0 * rows
        vb = v_ref[:, h0:h0 + HEAD_GROUP, :].reshape(grp_keys, HEAD_DIM).astype(BF16)
        raw = raws[g]
        probs = []

        if r_m > 0:
            sl = slice(base, base + r_m)
            raw_m, own_m = raw[0:r_m], own[0:r_m]
            s = jnp.where(own_m, raw_m * scale, -jnp.inf)
            m = jnp.max(s, axis=-1, keepdims=True)
            p = jnp.exp(s - m)
            pm_ref[r, sl] = bcast(m)
            pl_ref[r, sl] = bcast(jnp.sum(p, axis=-1, keepdims=True))
            pg_ref[r, sl] = bcast(jnp.sum(jnp.where(own_m, raw_m, 0.0), axis=-1, keepdims=True))
            probs.append(p)

        if r_s > r_m:
            sl = slice(base + r_m, base + r_s)
            n_r = r_s - r_m
            own_s = own[r_m:r_s]
            ls, lsn = _log_sigmoid_pair(raw[r_m:r_s] * scale)
            lsn = jnp.where(own_s, lsn, 0.0)
            stacked = jnp.concatenate([lsn[:, t * LANES:(t + 1) * LANES] for t in range(n_tiles)], axis=0)
            hi = stacked.astype(BF16)
            lo = (stacked - hi.astype(F32)).astype(BF16)
            intra = _dot(hi, utile_ref[...]) + _dot(lo, utile_ref[...])
            carry = carry_ref[sl][:, 0:1]
            suffix = carry
            later = [None] * n_tiles
            for t in reversed(range(n_tiles)):
                later[t] = intra[t * n_r:(t + 1) * n_r] + suffix
                suffix = suffix + jnp.sum(lsn[:, t * LANES:(t + 1) * LANES], axis=-1, keepdims=True)
            carry_ref[sl] = bcast(suffix)
            w = jnp.where(own_s, jnp.exp(ls + jnp.concatenate(later, axis=-1)), 0.0)
            probs.append(w)

        if r_s < grp_rows:
            sl = slice(base + r_s, base + grp_rows)
            own_f = own[r_s:]
            bias = fq_ref[sl][:, 0:1] - fexp_ref[g:g + 1, :]
            s = jnp.where(own_f, raw[r_s:] * scale + bias, -jnp.inf)
            m_old = m_ref[sl][:, 0:1]
            m_new = jnp.maximum(m_old, jnp.max(s, axis=-1, keepdims=True))
            alpha = jnp.exp(m_old - m_new)
            p = jnp.exp(s - m_new)
            l_ref[sl] = alpha * l_ref[sl] + bcast(jnp.sum(p, axis=-1, keepdims=True))
            m_ref[sl] = bcast(m_new)
            probs.append(p)

        out = _dot(jnp.concatenate(probs, axis=0).astype(BF16), vb)
        if r_m > 0:
            po_ref[r, base:base + r_m] = out[0:r_m]
        if r_s > r_m:
            sl = slice(base + r_m, base + r_s)
            acc_ref[sl] = acc_ref[sl] + out[r_m:r_s]
        if r_s < grp_rows:
            sl = slice(base + r_s, base + grp_rows)
            acc_ref[sl] = alpha * acc_ref[sl] + out[r_s:]

    @pl.when(r == n_pages - 1)
    def _():
        n_blocks = n_pages // pages_per_block
        moba_rows = h_moba * rows

        def fill(n, _):
            tot = pg_ref[n_pages - 1 - n * pages_per_block]
            for e in range(1, pages_per_block):
                tot = tot + pg_ref[n_pages - 1 - (n * pages_per_block + e)]
            gsb_ref[n] = tot * (1.0 / MOBA_BLOCK)
            return 0

        lax.fori_loop(0, n_blocks, fill, 0)

        def add_block(n, carry):
            m_run, l_run, o_run = carry
            gs_n = gsb_ref[n]

            def count(mb, rank):
                gs_m = gsb_ref[mb]
                beats = jnp.logical_or(gs_m > gs_n, jnp.logical_and(gs_m == gs_n, mb < n))
                return rank + jnp.where(beats, 1.0, 0.0)

            rank = lax.fori_loop(0, n_blocks, count, jnp.zeros((moba_rows, LANES), F32))
            chosen = rank < MOBA_TOPK
            for e in range(pages_per_block):
                slot = n_pages - 1 - (n * pages_per_block + e)
                m_p = jnp.where(chosen, pm_ref[slot], -jnp.inf)
                m_new = jnp.maximum(m_run, m_p)
                a_run = jnp.exp(m_run - m_new)
                a_p = jnp.exp(m_p - m_new)
                l_run = a_run * l_run + a_p * pl_ref[slot]
                o_run = a_run * o_run + a_p * po_ref[slot]
                m_run = m_new
            return m_run, l_run, o_run

        _, l_run, o_run = lax.fori_loop(0, n_blocks, add_block,
                                        (pm_ref[n_pages], pl_ref[n_pages], po_ref[n_pages]))
        moba_out = o_run / l_run
        for h in range(h_tot):
            hr = head_rows(h)
            if h < h_moba:
                res = moba_out[hr]
            elif h < h_moba + h_sb:
                res = acc_ref[hr]
            else:
                res = acc_ref[hr] / l_ref[hr]
            o_ref[:, h * HEAD_DIM:(h + 1) * HEAD_DIM] = res


def _sample_attention(q_s, k_new, v_new, logf_new, fcol_past, cache_k, cache_v, layer, page_table, heads):
    h_moba, h_sb, h_fox = heads
    h_tot = h_moba + h_sb + h_fox
    db, n_pages = page_table.shape
    n_new = q_s.shape[0] // db
    _, n_pool, page, hh, dd = cache_k.shape
    assert (page, hh, dd) == (LANES, h_tot, HEAD_DIM) and h_tot % HEAD_GROUP == 0
    assert n_new == SUBLANES and (n_pages * page) % MOBA_BLOCK == 0 and MOBA_BLOCK % page == 0
    n_groups = h_tot // HEAD_GROUP
    w = h_tot * HEAD_DIM
    n_rows = h_tot * n_new
    grp_keys = HEAD_GROUP * page

    zero_lane = LANES - 1
    assert h_fox < LANES
    lane_of_head = [h - h_moba - h_sb if h >= h_moba + h_sb else zero_lane for h in range(h_tot)]
    fexp = fcol_past[:, :, jnp.array(lane_of_head, jnp.int32)]
    fexp = fexp.reshape(db, n_pages, page, n_groups, HEAD_GROUP).transpose(0, 1, 3, 2, 4)
    fexp = fexp.reshape(db, n_pages, n_groups, grp_keys)
    ftot = fcol_past[:, n_pages * page - 1:, :]

    lane = jnp.arange(LANES, dtype=jnp.int32)
    utile = (lane[:, None] // HEAD_GROUP > lane[None, :] // HEAD_GROUP).astype(BF16)

    def row_spec(width):
        return pl.BlockSpec((n_new, width), lambda b, r, pt: (b, 0))

    def const_spec():
        return pl.BlockSpec((LANES, LANES), lambda b, r, pt: (0, 0))

    cache_spec = pl.BlockSpec((None, None, page, h_tot, HEAD_DIM),
                              lambda b, r, pt: (layer, pt[b, n_pages - 1 - r], 0, 0, 0))

    grid_spec = pltpu.PrefetchScalarGridSpec(
        num_scalar_prefetch=1,
        grid=(db, n_pages),
        in_specs=[row_spec(w), row_spec(w), row_spec(w), row_spec(LANES),
                  pl.BlockSpec((None, None, n_groups, grp_keys), lambda b, r, pt: (b, n_pages - 1 - r, 0, 0)),
                  pl.BlockSpec((None, 1, LANES), lambda b, r, pt: (b, 0, 0)),
                  const_spec(), const_spec()]
                 + [cache_spec, cache_spec],
        out_specs=row_spec(w),
        scratch_shapes=[
            pltpu.VMEM((page, w), F32), pltpu.VMEM((page, w), F32),
            pltpu.VMEM((n_rows, HEAD_DIM), F32),
            pltpu.VMEM((n_rows, LANES), F32),
            pltpu.VMEM((n_rows, LANES), F32),
            pltpu.VMEM((n_rows, LANES), F32),
            pltpu.VMEM((n_rows, LANES), F32),
            pltpu.VMEM((LANES, LANES), F32),
            pltpu.VMEM((n_pages + 1, h_moba * n_new, LANES), F32),
            pltpu.VMEM((n_pages + 1, h_moba * n_new, LANES), F32),
            pltpu.VMEM((n_pages + 1, h_moba * n_new, HEAD_DIM), F32),
            pltpu.VMEM((n_pages, h_moba * n_new, LANES), F32),
            pltpu.VMEM((n_pages // (MOBA_BLOCK // page), h_moba * n_new, LANES), F32),
        ],
    )
    body = functools.partial(_sample_attn_body, heads=heads, n_groups=n_groups, n_pages=n_pages,
                             scale=HEAD_DIM ** -0.5, n_new=n_new)
    return pl.pallas_call(
        body,
        grid_spec=grid_spec,
        out_shape=jax.ShapeDtypeStruct((db * n_new, w), F32),
        compiler_params=_cparams("parallel", "arbitrary"),
        name="sample_attn",
    )(page_table, q_s, k_new, v_new, logf_new, fexp, ftot, _tri(page, "row_gt_col"), utile,
      cache_k, cache_v)


def _layer_common(x, w, heads, cos_t, sin_t):
    h = _mm_resid(_ffn_up(_rmsnorm(x, w["g_f1"]), w["f1_gate"], w["f1_up"]), w["f1_down"], x, 0.5)
    n = _rmsnorm(h, w["g_mix"])
    w_mix = sum(heads) * HEAD_DIM
    d = x.shape[1]
    t0 = w_mix // PROJ_TN
    w_in, layer = w["w_in"], w["layer"]
    q_bf, = _proj_qk(n, w_in, layer, 0, w["q_gains"], cos_t, sin_t, heads, False)
    k_bf, k_f32 = _proj_qk(n, w_in, layer, t0, w["k_gains"], cos_t, sin_t, heads, True)
    v_bf, v_f32 = _proj_v(n, w_in, layer, 2 * t0, w_mix)
    gates = _proj_gate(n, w_in, layer, 3 * t0, 3 * d)
    logf = _proj_logf(n, w_in, layer, 3 * t0 + 3 * d // PROJ_TN, w["b_f"], heads[2])
    return h, q_bf, k_bf, k_f32, v_bf, v_f32, gates, logf


def _layer_tail(h, o_a, o_b, o_c, gates, w):
    merged = _merge(o_a, o_b, o_c, w["wb_a"], w["wb_b"], w["wb_c"], gates)
    h = _mm_resid(merged, w["w_out"], h, 1.0)
    return _mm_resid(_ffn_up(_rmsnorm(h, w["g_f2"]), w["f2_gate"], w["f2_up"]), w["f2_down"], h, 0.5)


def kernel(x_prompt, x_sample, cache_k, cache_v, cache_logf, page_table, norm_ffn1, ffn1_w_gate, ffn1_w_up, ffn1_w_down, norm_mix, w_in, b_forget, q_norm_moba, k_norm_moba, q_norm_fox, k_norm_fox, w_branch_moba, w_branch_sb, w_branch_fox, w_out, norm_ffn2, ffn2_w_gate, ffn2_w_up, ffn2_w_down):
    batch, t, d = x_prompt.shape
    db, n_new, _ = x_sample.shape
    depth = norm_ffn1.shape[0]
    h_fox = b_forget.shape[1]
    h_moba = w_branch_moba.shape[1] // HEAD_DIM
    h_sb = w_branch_sb.shape[1] // HEAD_DIM
    heads = (h_moba, h_sb, h_fox)
    h_tot = sum(heads)
    w_mix = h_tot * HEAD_DIM
    past_len = page_table.shape[1] * cache_k.shape[2]
    assert past_len % MOBA_BLOCK == 0 and n_new <= MOBA_BLOCK
    assert (3 * w_mix + 3 * d) % PROJ_TN == 0 and h_fox <= LANES

    d_ff = ffn1_w_gate.shape[2]
    ff_pad = -d_ff % 512

    def up(wt):
        return jnp.pad(wt.astype(BF16), ((0, 0), (0, 0), (0, ff_pad)))

    def down(wt):
        return jnp.pad(wt.astype(BF16), ((0, 0), (0, ff_pad), (0, 0)))

    f1g, f1u, f1d = up(ffn1_w_gate), up(ffn1_w_up), down(ffn1_w_down)
    f2g, f2u, f2d = up(ffn2_w_gate), up(ffn2_w_up), down(ffn2_w_down)
    w_in_b = w_in.astype(BF16)
    wba, wbb, wbc, wo = (a.astype(BF16) for a in (w_branch_moba, w_branch_sb, w_branch_fox, w_out))
    b_pad = jnp.pad(b_forget, ((0, 0), (0, LANES - h_fox)))

    cos_p, sin_p = _rope_tables(jnp.arange(t, dtype=jnp.int32))
    cos_s, sin_s = _rope_tables(past_len + jnp.arange(n_new, dtype=jnp.int32))
    cos_s, sin_s = jnp.tile(cos_s, (db, 1)), jnp.tile(sin_s, (db, 1))

    yp = x_prompt.reshape(batch * t, d)
    ys = x_sample.reshape(db * n_new, d)
    outs = [[] for _ in range(6)]
    for l in range(depth):
        w = dict(g_f1=norm_ffn1[l], f1_gate=f1g[l], f1_up=f1u[l], f1_down=f1d[l], g_mix=norm_mix[l],
                 w_in=w_in_b, layer=l, b_f=b_pad[l:l + 1],
                 q_gains=jnp.stack([q_norm_moba[l], q_norm_fox[l]]).reshape(2, 1, HEAD_DIM),
                 k_gains=jnp.stack([k_norm_moba[l], k_norm_fox[l]]).reshape(2, 1, HEAD_DIM),
                 wb_a=wba[l], wb_b=wbb[l], wb_c=wbc[l], w_out=wo[l],
                 g_f2=norm_ffn2[l], f2_gate=f2g[l], f2_up=f2u[l], f2_down=f2d[l])

        h, q_bf, k_bf, k_f32, v_bf, v_f32, gates, logf = _layer_common(yp, w, heads, cos_p, sin_p)
        fcol, frow = _cumsum_prompt(logf, batch, t)
        o_a, o_b, o_c = _prompt_attention(q_bf, k_bf, v_bf, fcol, frow, batch, t, heads)
        yp = _layer_tail(h, o_a, o_b, o_c, gates, w)
        outs[0].append(k_f32.reshape(batch, t, h_tot, HEAD_DIM))
        outs[1].append(v_f32.reshape(batch, t, h_tot, HEAD_DIM))
        outs[2].append(logf[:, :h_fox].reshape(batch, t, h_fox))

        h, q_bf, k_bf, k_f32, v_bf, v_f32, gates, logf = _layer_common(ys, w, heads, cos_s, sin_s)
        fcol_past = _cumsum_pages(cache_logf, l, page_table)
        o = _sample_attention(q_bf.astype(F32), k_f32, v_f32, logf, fcol_past, cache_k, cache_v, l,
                              page_table, heads)
        a_end, b_end = h_moba * HEAD_DIM, (h_moba + h_sb) * HEAD_DIM
        ys = _layer_tail(h, o[:, :a_end], o[:, a_end:b_end], o[:, b_end:], gates, w)
        outs[3].append(k_f32.reshape(db, n_new, h_tot, HEAD_DIM))
        outs[4].append(v_f32.reshape(db, n_new, h_tot, HEAD_DIM))
        outs[5].append(logf[:, :h_fox].reshape(db, n_new, h_fox))

    return (yp.reshape(batch, t, d), ys.reshape(db, n_new, d)) + tuple(jnp.stack(o) for o in outs)
```
